```python
import math
import jax, jax.numpy as jnp
from jax import lax
import numpy as np

D_MODEL = 1024
BATCH = 8
SEQ = 8192
DEPTH = 2

MIX_WIDTH = D_MODEL // 2
RET_HEAD_DIM = 128
RET_HEADS = MIX_WIDTH // RET_HEAD_DIM
RET_CHUNK = 128
ROPE_BASE = 10000.0
RWKV_HEAD_DIM = 64
RWKV_HEADS = MIX_WIDTH // RWKV_HEAD_DIM
DECAY_LORA = 64
AAA_LORA = 64
GATE_LORA = 128
CONV_WIDTH = 3
N_BRANCHES = 3
RET_COLS = 4 * MIX_WIDTH
RWKV_COLS = 3 * MIX_WIDTH + DECAY_LORA + AAA_LORA + GATE_LORA
CONV_COLS = 3 * MIX_WIDTH
IN_COLS = RET_COLS + RWKV_COLS + CONV_COLS
N_GROUPS = 4
EXPERTS_PER_GROUP = 8
N_EXPERTS = N_GROUPS * EXPERTS_PER_GROUP
EXPERT_FF = D_MODEL // 2
TOP_K = 2
MOE_BLOCK = 128
NORM_EPS = 1e-6
HEAD_NORM_EPS = 1e-5
RWKV_NORM_EPS = 64e-5

kernel_name = 'hybrid_retention_rwkv7_shortconv_hmoe_block'

F32 = jnp.float32


def rms_norm(x, g):
    xf = x.astype(F32)
    y = xf * lax.rsqrt(jnp.mean(xf * xf, axis=-1, keepdims=True) + NORM_EPS)
    return (y * g.astype(F32)).astype(x.dtype)


def modulate(h, shift, scale):
    return (h.astype(F32) * (1.0 + scale[:, None, :]) + shift[:, None, :]).astype(h.dtype)


def head_norm(y, g, eps):
    yf = y.astype(F32)
    mu = jnp.mean(yf, axis=-1, keepdims=True)
    var = jnp.mean(jnp.square(yf - mu), axis=-1, keepdims=True)
    return (yf - mu) * lax.rsqrt(var + eps) * g.astype(F32).reshape(y.shape[-2:])


def rotary(t, positions):
    half = t.shape[-1] // 2
    inv_freq = ROPE_BASE ** (-jnp.arange(half, dtype=F32) / half)
    ang = positions.astype(F32)[..., None] * inv_freq
    cos = jnp.cos(ang)[:, :, None, :]
    sin = jnp.sin(ang)[:, :, None, :]
    t1 = t[..., :half].astype(F32)
    t2 = t[..., half:].astype(F32)
    return jnp.concatenate([t1 * cos - t2 * sin, t1 * sin + t2 * cos], axis=-1)


def retention(q, k, v, g, positions, gn_g):
    bsz, seq, _ = q.shape
    split = lambda t: t.reshape(bsz, seq, RET_HEADS, RET_HEAD_DIM)
    qh = rotary(split(q), positions)
    kh = rotary(split(k), positions) * (RET_HEAD_DIM ** -0.5)
    vh = split(v).astype(F32)
    n_chunks = seq // RET_CHUNK

    def to_chunks(t):
        return t.reshape(bsz, n_chunks, RET_CHUNK, RET_HEADS, RET_HEAD_DIM).transpose(1, 0, 3, 2, 4)

    log_gamma = jnp.log1p(-jnp.exp2(-5.0 - jnp.arange(RET_HEADS, dtype=F32)))
    pos = jnp.arange(RET_CHUNK, dtype=F32)
    rel = pos[:, None] - pos[None, :]
    decay_in = jnp.where(rel >= 0, jnp.exp(log_gamma[:, None, None] * jnp.maximum(rel, 0.0)), 0.0)
    decay_k = jnp.exp(log_gamma[:, None] * (RET_CHUNK - 1 - pos))
    decay_q = jnp.exp(log_gamma[:, None] * (pos + 1.0))
    decay_chunk = jnp.exp(log_gamma * RET_CHUNK)

    def chunk_step(state, qkv):
        qc, kc, vc = qkv
        scores = jnp.einsum('bhid,bhjd->bhij', qc, kc) * decay_in
        inner = jnp.einsum('bhij,bhjv->bhiv', scores, vc)
        cross = jnp.einsum('bhid,bhdv->bhiv', qc, state) * decay_q[None, :, :, None]
        state = state * decay_chunk[None, :, None, None] + jnp.einsum(
            'bhjd,bhjv->bhdv', kc * decay_k[None, :, :, None], vc)
        return state, inner + cross

    state0 = jnp.zeros((bsz, RET_HEADS, RET_HEAD_DIM, RET_HEAD_DIM), F32)
    _, out = lax.scan(chunk_step, state0, (to_chunks(qh), to_chunks(kh), to_chunks(vh)))
    out = out.transpose(1, 0, 3, 2, 4).reshape(bsz, seq, RET_HEADS, RET_HEAD_DIM)
    out = head_norm(out, gn_g, HEAD_NORM_EPS).reshape(bsz, seq, MIX_WIDTH)
    return (jax.nn.silu(g.astype(F32)) * out).astype(q.dtype)


def rwkv7_time_mix(p, mu, w0, w_lora, a0, a_lora, g_lora, k_k, k_a, r_k, gn_g):
    bsz, seq, _ = p.shape
    mw = MIX_WIDTH
    pf = p.astype(F32)
    prev = jnp.pad(pf, ((0, 0), (1, 0), (0, 0)))[:, :-1]
    pf = pf + mu * (prev - pf)
    o_w = 3 * mw
    o_a = o_w + DECAY_LORA
    o_g = o_a + AAA_LORA
    r = pf[..., 0:mw]
    k = pf[..., mw:2 * mw]
    v = pf[..., 2 * mw:3 * mw]
    wd = pf[..., o_w:o_a]
    ad = pf[..., o_a:o_g]
    gd = pf[..., o_g:o_g + GATE_LORA]
    w = jnp.exp(-math.exp(-0.5) * jax.nn.sigmoid(w0 + jnp.tanh(wd) @ w_lora))
    a = jax.nn.sigmoid(a0 + ad @ a_lora)
    g = jax.nn.sigmoid(gd) @ g_lora
    heads = lambda t: t.reshape(bsz, seq, RWKV_HEADS, RWKV_HEAD_DIM)
    r, w, k, v, a = heads(r), heads(w), heads(k), heads(v), heads(a)
    kk = k * k_k.reshape(RWKV_HEADS, RWKV_HEAD_DIM)
    kh = kk * lax.rsqrt(jnp.maximum(jnp.sum(kk * kk, axis=-1, keepdims=True), 1e-12))
    kt = k * (1.0 + (a - 1.0) * k_a.reshape(RWKV_HEADS, RWKV_HEAD_DIM))

    def time_step(state, inp):
        rt, wt, ktt, vt, kht, at = inp
        sk = jnp.einsum('bhvk,bhk->bhv', state, kht)
        state = (state * wt[:, :, None, :]
                 - sk[..., None] * (at * kht)[:, :, None, :]
                 + vt[..., None] * ktt[:, :, None, :])
        return state, jnp.einsum('bhvk,bhk->bhv', state, rt)

    tm = lambda t: jnp.moveaxis(t, 1, 0)
    state0 = jnp.zeros((bsz, RWKV_HEADS, RWKV_HEAD_DIM, RWKV_HEAD_DIM), F32)
    _, y = lax.scan(time_step, state0, (tm(r), tm(w), tm(kt), tm(v), tm(kh), tm(a)))
    y = jnp.moveaxis(y, 0, 1)
    y = head_norm(y, gn_g, RWKV_NORM_EPS)
    y = y + jnp.sum(r * kt * r_k.astype(F32), axis=-1, keepdims=True) * v
    return (y.reshape(bsz, seq, mw) * g).astype(p.dtype)


def short_conv(p, conv_w):
    h, bgate, cgate = jnp.split(p, 3, axis=-1)
    u = cgate * h
    seq = u.shape[1]
    up = jnp.pad(u, ((0, 0), (CONV_WIDTH - 1, 0), (0, 0)))
    conv = conv_w[0] * up[:, 0:seq]
    for i in range(1, CONV_WIDTH):
        conv = conv + conv_w[i] * up[:, i:i + seq]
    return bgate * conv


def hier_moe(h, w_router_group, b_router_group, w_router_expert, b_router_expert,
             w_exp_gate, w_exp_up, w_exp_down):
    bsz, seq, d = h.shape
    n_tok = bsz * seq
    xf = h.reshape(n_tok, d)
    g_logits = (xf @ w_router_group + b_router_group).astype(F32)
    g_prob = jax.nn.softmax(g_logits, axis=-1)
    g_sel = jnp.argmax(g_logits, axis=-1).astype(jnp.int32)
    p_group = jnp.take_along_axis(g_prob, g_sel[:, None], axis=-1)
    e_logits = (xf @ w_router_expert + b_router_expert).astype(F32).reshape(n_tok, N_GROUPS, EXPERTS_PER_GROUP)
    e_logits = jnp.take_along_axis(e_logits, g_sel[:, None, None], axis=1)[:, 0]
    top_v, top_i = lax.top_k(e_logits, TOP_K)
    weights = (jax.nn.softmax(top_v, axis=-1) * p_group).reshape(-1)
    eid = (g_sel[:, None] * EXPERTS_PER_GROUP + top_i).reshape(-1).astype(jnp.int32)
    tok = jnp.repeat(jnp.arange(n_tok, dtype=jnp.int32), TOP_K)
    n_assign = n_tok * TOP_K
    order = jnp.argsort(eid)
    eid_s, tok_s, w_s = eid[order], tok[order], weights[order]
    counts = jnp.bincount(eid, length=N_EXPERTS).astype(jnp.int32)
    starts = jnp.cumsum(counts) - counts
    padded = (counts + MOE_BLOCK - 1) // MOE_BLOCK * MOE_BLOCK
    pends = jnp.cumsum(padded)
    pstarts = pends - padded
    dest = pstarts[eid_s] + jnp.arange(n_assign, dtype=jnp.int32) - starts[eid_s]
    n_rows = n_assign + N_EXPERTS * MOE_BLOCK
    n_blocks = n_rows // MOE_BLOCK
    row_tok = jnp.full((n_rows,), n_tok, jnp.int32).at[dest].set(tok_s)
    row_w = jnp.zeros((n_rows,), F32).at[dest].set(w_s)
    block_e = jnp.minimum(
        jnp.searchsorted(pends, jnp.arange(n_blocks, dtype=jnp.int32) * MOE_BLOCK, side='right'),
        N_EXPERTS - 1)
    x_rows = jnp.concatenate([xf, jnp.zeros((1, d), xf.dtype)], axis=0)[row_tok]
    x_rows = x_rows.reshape(n_blocks, MOE_BLOCK, d)

    def expert_block(args):
        xb, e = args
        return (jax.nn.silu(xb @ w_exp_gate[e]) * (xb @ w_exp_up[e])) @ w_exp_down[e]

    y_rows = lax.map(expert_block, (x_rows, block_e)).reshape(n_rows, d)
    y_rows = y_rows.astype(F32) * row_w[:, None]
    out = jax.ops.segment_sum(y_rows, row_tok, num_segments=n_tok + 1)[:n_tok]
    return out.reshape(bsz, seq, d).astype(h.dtype)


def decoder_layer(x, c, positions, norm1_g, norm2_g, w_ada, b_ada, w_in, w_gate, b_gate,
                  ret_gn_g, rwkv_mu, rwkv_w0, rwkv_w_lora, rwkv_a0, rwkv_a_lora, rwkv_g_lora,
                  rwkv_k_k, rwkv_k_a, rwkv_r_k, rwkv_gn_g, conv_w, w_branch, w_o,
                  w_router_group, b_router_group, w_router_expert, b_router_expert,
                  w_exp_gate, w_exp_up, w_exp_down):
    d = D_MODEL
    mod = jax.nn.silu(c.astype(F32)) @ w_ada + b_ada
    shift1, scale1, gate1, shift2, scale2, gate2 = jnp.split(mod, 6, axis=-1)

    h = modulate(rms_norm(x, norm1_g), shift1, scale1)
    p = h @ w_in
    p_ret = p[..., :RET_COLS]
    p_rwkv = p[..., RET_COLS:RET_COLS + RWKV_COLS]
    p_conv = p[..., RET_COLS + RWKV_COLS:]
    q, k, v, g = jnp.split(p_ret, 4, axis=-1)
    y_ret = retention(q, k, v, g, positions, ret_gn_g)
    y_rwkv = rwkv7_time_mix(p_rwkv, rwkv_mu, rwkv_w0, rwkv_w_lora, rwkv_a0, rwkv_a_lora,
                            rwkv_g_lora, rwkv_k_k, rwkv_k_a, rwkv_r_k, rwkv_gn_g)
    y_conv = short_conv(p_conv, conv_w)
    gates = jax.nn.sigmoid((h @ w_gate + b_gate).astype(F32))
    merged = jnp.zeros(x.shape, F32)
    for i, y in enumerate((y_ret, y_rwkv, y_conv)):
        merged = merged + gates[..., i * d:(i + 1) * d] * (y @ w_branch[i])
    x = x + (gate1[:, None, :] * (merged.astype(x.dtype) @ w_o)).astype(x.dtype)

    h2 = modulate(rms_norm(x, norm2_g), shift2, scale2)
    y_moe = hier_moe(h2, w_router_group, b_router_group, w_router_expert, b_router_expert,
                     w_exp_gate, w_exp_up, w_exp_down)
    x = x + (gate2[:, None, :] * y_moe).astype(x.dtype)
    return x


def setup_inputs(seed: int = 0) -> dict:
    key = jax.random.key(seed)
    k = jax.random.split(key, 32)
    L, D, MW = DEPTH, D_MODEL, MIX_WIDTH

    def nrm(kk, shape, scale):
        return scale * jax.random.normal(kk, shape, jnp.float32)

    return {
        'x': nrm(k[0], (BATCH, SEQ, D), 1.0),
        'c': nrm(k[1], (BATCH, D), 1.0),
        'positions': jnp.tile(jnp.arange(SEQ, dtype=jnp.int32)[None, :], (BATCH, 1)),
        'norm1_g': 1.0 + nrm(k[2], (L, D), 0.02),
        'norm2_g': 1.0 + nrm(k[3], (L, D), 0.02),
        'final_norm_g': 1.0 + nrm(k[4], (D,), 0.02),
        'w_ada': nrm(k[5], (L, D, 6 * D), 0.5 * D ** -0.5),
        'b_ada': nrm(k[6], (L, 6 * D), 0.01),
        'w_in': nrm(k[7], (L, D, IN_COLS), D ** -0.5),
        'w_gate': nrm(k[8], (L, D, N_BRANCHES * D), D ** -0.5),
        'b_gate': nrm(k[9], (L, N_BRANCHES * D), 0.01),
        'ret_gn_g': 1.0 + nrm(k[10], (L, MW), 0.02),
        'rwkv_mu': jax.random.uniform(k[11], (L, RWKV_COLS), jnp.float32),
        'rwkv_w0': nrm(k[12], (L, MW), 0.5),
        'rwkv_w_lora': nrm(k[13], (L, DECAY_LORA, MW), DECAY_LORA ** -0.5),
        'rwkv_a0': nrm(k[14], (L, MW), 0.5),
        'rwkv_a_lora': nrm(k[15], (L, AAA_LORA, MW), AAA_LORA ** -0.5),
        'rwkv_g_lora': nrm(k[16], (L, GATE_LORA, MW), GATE_LORA ** -0.5),
        'rwkv_k_k': 0.85 + nrm(k[17], (L, MW), 0.05),
        'rwkv_k_a': 1.0 + nrm(k[18], (L, MW), 0.05),
        'rwkv_r_k': nrm(k[19], (L, RWKV_HEADS, RWKV_HEAD_DIM), 0.1),
        'rwkv_gn_g': 1.0 + nrm(k[20], (L, MW), 0.02),
        'conv_w': nrm(k[21], (L, CONV_WIDTH, MW), CONV_WIDTH ** -0.5),
        'w_branch': nrm(k[22], (L, N_BRANCHES, MW, D), MW ** -0.5),
        'w_o': nrm(k[23], (L, D, D), D ** -0.5),
        'w_router_group': nrm(k[24], (L, D, N_GROUPS), D ** -0.5),
        'b_router_group': nrm(k[25], (L, N_GROUPS), 0.01),
        'w_router_expert': nrm(k[26], (L, D, N_EXPERTS), D ** -0.5),
        'b_router_expert': nrm(k[27], (L, N_EXPERTS), 0.01),
        'w_exp_gate': nrm(k[28], (L, N_EXPERTS, D, EXPERT_FF), D ** -0.5),
        'w_exp_up': nrm(k[29], (L, N_EXPERTS, D, EXPERT_FF), D ** -0.5),
        'w_exp_down': nrm(k[30], (L, N_EXPERTS, EXPERT_FF, D), EXPERT_FF ** -0.5),
    }


def reference(x, c, positions, norm1_g, norm2_g, final_norm_g, w_ada, b_ada, w_in, w_gate, b_gate,
              ret_gn_g, rwkv_mu, rwkv_w0, rwkv_w_lora, rwkv_a0, rwkv_a_lora, rwkv_g_lora,
              rwkv_k_k, rwkv_k_a, rwkv_r_k, rwkv_gn_g, conv_w, w_branch, w_o,
              w_router_group, b_router_group, w_router_expert, b_router_expert,
              w_exp_gate, w_exp_up, w_exp_down):
    for l in range(DEPTH):
        x = decoder_layer(
            x, c, positions, norm1_g[l], norm2_g[l], w_ada[l], b_ada[l], w_in[l], w_gate[l], b_gate[l],
            ret_gn_g[l], rwkv_mu[l], rwkv_w0[l], rwkv_w_lora[l], rwkv_a0[l], rwkv_a_lora[l],
            rwkv_g_lora[l], rwkv_k_k[l], rwkv_k_a[l], rwkv_r_k[l], rwkv_gn_g[l], conv_w[l],
            w_branch[l], w_o[l], w_router_group[l], b_router_group[l], w_router_expert[l],
            b_router_expert[l], w_exp_gate[l], w_exp_up[l], w_exp_down[l])
    return rms_norm(x, final_norm_g)
```

```python
import functools
import math

import jax
import jax.numpy as jnp
from jax import lax
from jax.experimental import pallas as pl
from jax.experimental.pallas import tpu as pltpu

F32 = jnp.float32
BF16 = jnp.bfloat16
HI = lax.Precision.HIGHEST

D_MODEL = 1024
MIX_WIDTH = D_MODEL // 2
RET_HEAD_DIM = 128
RET_HEADS = MIX_WIDTH // RET_HEAD_DIM
RET_CHUNK = 128
ROPE_BASE = 10000.0
RWKV_HEAD_DIM = 64
RWKV_HEADS = MIX_WIDTH // RWKV_HEAD_DIM
RWKV_PAIRS = RWKV_HEADS // 2
RWKV_CHUNK = 64
DECAY_LORA = 64
AAA_LORA = 64
GATE_LORA = 128
LORA_COLS = DECAY_LORA + AAA_LORA + GATE_LORA
N_BRANCHES = 3
RET_COLS = 4 * MIX_WIDTH
CONV_COLS = 3 * MIX_WIDTH
RWKV_COLS = 3 * MIX_WIDTH + LORA_COLS
IN_COLS = RET_COLS + RWKV_COLS + CONV_COLS
GATE_COLS = N_BRANCHES * D_MODEL
N_GROUPS = 4
EXPERTS_PER_GROUP = 8
N_EXPERTS = N_GROUPS * EXPERTS_PER_GROUP
EXPERT_FF = D_MODEL // 2
NORM_EPS = 1e-6
HEAD_NORM_EPS = 1e-5
RWKV_NORM_EPS = 64e-5

LANES = 128
SUBLANES = 8
VMEM_LIMIT = 56 * 1024 * 1024

COL_GATE = 0
COL_RET = GATE_COLS
COL_CONV = COL_RET + RET_COLS
COL_RWKV = COL_CONV + CONV_COLS
COL_LORA = COL_RWKV + 3 * MIX_WIDTH
PROJ_COLS = IN_COLS + GATE_COLS

PROJ_TM = 1024
PROJ_TN = 768
RET_TB = 512
RWKV_TB = 256
MERGE_TM = 256
ROW_TILE = 256
EXPERT_ROWS = 256
ROUTE_LANE0 = N_GROUPS


def _params(sem, vmem=VMEM_LIMIT):
    return pltpu.CompilerParams(dimension_semantics=sem, vmem_limit_bytes=vmem)


def _nt(a, b):
    return lax.dot_general(a, b, (((1,), (1,)), ((), ())), preferred_element_type=F32)


def _tn(a, b):
    return lax.dot_general(a, b, (((0,), (0,)), ((), ())), preferred_element_type=F32)


def _mm(a, b):
    return jnp.dot(a, b, preferred_element_type=F32)


def _mm_hi(a, b):
    return jnp.dot(a, b, precision=HI, preferred_element_type=F32)


def _ada_kernel(c_ref, w_ref, b_ref, o_ref):
    c = c_ref[...]
    s = c * jax.nn.sigmoid(c)
    o_ref[0] = _mm_hi(s, w_ref[0]) + b_ref[0]


def _ada(c, w_ada, b_ada):
    n_l, d, d6 = w_ada.shape
    bsz = c.shape[0]
    return pl.pallas_call(
        _ada_kernel,
        grid=(n_l, d6 // d),
        in_specs=[pl.BlockSpec((bsz, d), lambda l, j: (0, 0)),
                  pl.BlockSpec((1, d, d), lambda l, j: (l, 0, j)),
                  pl.BlockSpec((1, 1, d), lambda l, j: (l, 0, j))],
        out_specs=pl.BlockSpec((1, bsz, d), lambda l, j: (l, 0, j)),
        out_shape=jax.ShapeDtypeStruct((n_l, bsz, d6), F32),
        compiler_params=_params(("arbitrary", "arbitrary")),
        name="ada",
    )(c, w_ada, b_ada.reshape(n_l, 1, d6))


def _rope_kernel(pos_ref, invf_ref, sign_ref, cos_ref, sin_ref):
    ang = pos_ref[...] * invf_ref[...]
    cos_ref[...] = jnp.cos(ang)
    sin_ref[...] = jnp.sin(ang) * sign_ref[...]


def _rope_tables(positions):
    n_tok = positions.size
    half = RET_HEAD_DIM // 2
    inv_freq = ROPE_BASE ** (-jnp.arange(half, dtype=F32) / half)
    invf = jnp.concatenate([inv_freq, inv_freq])[None, :]
    sign = jnp.concatenate([-jnp.ones((half,), F32), jnp.ones((half,), F32)])[None, :]
    pos = jnp.broadcast_to(positions.astype(F32).reshape(n_tok, 1), (n_tok, RET_HEAD_DIM))
    tm = 1024
    row = pl.BlockSpec((tm, RET_HEAD_DIM), lambda i: (i, 0))
    const = pl.BlockSpec((1, RET_HEAD_DIM), lambda i: (0, 0))
    return pl.pallas_call(
        _rope_kernel,
        grid=(n_tok // tm,),
        in_specs=[row, const, const],
        out_specs=[row, row],
        out_shape=[jax.ShapeDtypeStruct((n_tok, RET_HEAD_DIM), F32)] * 2,
        compiler_params=_params(("arbitrary",)),
        name="rope",
    )(pos, invf, sign)


def _norm_mod(x, g, shift, scale):
    ms = jnp.mean(x * x, axis=-1, keepdims=True)
    return (x * lax.rsqrt(ms + NORM_EPS) * g) * (1.0 + scale) + shift


def _proj_kernel(x_ref, g_ref, shift_ref, scale_ref, w_ref, b_ref, o_ref, h_ref, *, n_gate):
    j = pl.program_id(2)

    @pl.when(j == 0)
    def _():
        h_ref[...] = _norm_mod(x_ref[...], g_ref[...], shift_ref[0], scale_ref[0]).astype(BF16)

    acc = _mm(h_ref[...], w_ref[...]) + b_ref[...]

    @pl.when(j < n_gate)
    def _():
        o_ref[...] = jax.nn.sigmoid(acc)

    @pl.when(j >= n_gate)
    def _():
        o_ref[...] = acc


def _proj(x, norm_g, shift, scale, w_cat, b_cat, bsz, seq):
    d = x.shape[-1]
    tm = min(PROJ_TM, seq)
    nt = seq // tm
    return pl.pallas_call(
        functools.partial(_proj_kernel, n_gate=GATE_COLS // PROJ_TN),
        grid=(bsz, nt, PROJ_COLS // PROJ_TN),
        in_specs=[pl.BlockSpec((tm, d), lambda b, i, j: (b * nt + i, 0)),
                  pl.BlockSpec((1, d), lambda b, i, j: (0, 0)),
                  pl.BlockSpec((1, 1, d), lambda b, i, j: (b, 0, 0)),
                  pl.BlockSpec((1, 1, d), lambda b, i, j: (b, 0, 0)),
                  pl.BlockSpec((d, PROJ_TN), lambda b, i, j: (0, j)),
                  pl.BlockSpec((1, PROJ_TN), lambda b, i, j: (0, j))],
        out_specs=pl.BlockSpec((tm, PROJ_TN), lambda b, i, j: (b * nt + i, j)),
        out_shape=jax.ShapeDtypeStruct((bsz * seq, PROJ_COLS), F32),
        scratch_shapes=[pltpu.VMEM((tm, d), BF16)],
        compiler_params=_params(("arbitrary", "arbitrary", "arbitrary")),
        name="proj",
    )(x, norm_g, shift, scale, w_cat, b_cat)


def _ret_tables():
    log_gamma = jnp.log1p(-jnp.exp2(-5.0 - jnp.arange(RET_HEADS, dtype=F32)))
    pos = jnp.arange(RET_CHUNK, dtype=F32)
    rel = pos[:, None] - pos[None, :]
    decay_in = jnp.where(rel >= 0, jnp.exp(log_gamma[:, None, None] * jnp.maximum(rel, 0.0)), 0.0)
    decay_k = jnp.exp(log_gamma[:, None] * (RET_CHUNK - 1 - pos))
    decay_q = jnp.exp(log_gamma[:, None] * (pos + 1.0))
    decay_chunk = jnp.exp(log_gamma * RET_CHUNK)
    bcast = lambda t: jnp.broadcast_to(t[:, :, None], (RET_HEADS, RET_CHUNK, RET_HEAD_DIM))
    dc = jnp.broadcast_to(decay_chunk[:, None, None], (RET_HEADS, 1, RET_HEAD_DIM))
    return decay_in, bcast(decay_k), bcast(decay_q), dc


def _ret_kernel(q_ref, k_ref, v_ref, g_ref, cos_ref, sin_ref, din_ref, zeta_ref, xi_ref, dc_ref, gn_ref,
                o_ref, state_ref, *, tb):
    @pl.when(pl.program_id(1) == 0)
    def _():
        state_ref[...] = jnp.zeros_like(state_ref)

    k_scale = RET_HEAD_DIM ** -0.5
    for c in range(tb // RET_CHUNK):
        rows = slice(c * RET_CHUNK, (c + 1) * RET_CHUNK)
        cosf = cos_ref[rows, :]
        sinf = sin_ref[rows, :]
        for h in range(RET_HEADS):
            cols = slice(h * RET_HEAD_DIM, (h + 1) * RET_HEAD_DIM)
            q = q_ref[rows, cols]
            k = k_ref[rows, cols]
            qr = q * cosf + pltpu.roll(q, RET_HEAD_DIM // 2, 1) * sinf
            kr = (k * cosf + pltpu.roll(k, RET_HEAD_DIM // 2, 1) * sinf) * k_scale
            qb = qr.astype(BF16)
            vb = v_ref[rows, cols].astype(BF16)
            scores = _nt(qb, kr.astype(BF16)) * din_ref[h]
            inner = _mm(scores.astype(BF16), vb)
            st = state_ref[h]
            cross = _mm(qb, st.astype(BF16)) * xi_ref[h]
            state_ref[h] = st * dc_ref[h] + _tn((kr * zeta_ref[h]).astype(BF16), vb)
            out = inner + cross
            mu = jnp.mean(out, axis=-1, keepdims=True)
            dev = out - mu
            var = jnp.mean(dev * dev, axis=-1, keepdims=True)
            g = g_ref[rows, cols]
            o_ref[rows, cols] = (g * jax.nn.sigmoid(g)) * (dev * lax.rsqrt(var + HEAD_NORM_EPS) * gn_ref[:, cols])


def _retention(pg, cosf, sinf, tables, gn_g, bsz, seq):
    tb = min(RET_TB, seq)
    nt = seq // tb
    mw = MIX_WIDTH
    col = lambda cb: pl.BlockSpec((tb, mw), lambda b, i, cb=cb: (b * nt + i, cb))
    rope = pl.BlockSpec((tb, RET_HEAD_DIM), lambda b, i: (b * nt + i, 0))
    tab = pl.BlockSpec((RET_HEADS, RET_CHUNK, RET_HEAD_DIM), lambda b, i: (0, 0, 0))
    c0 = COL_RET // mw
    return pl.pallas_call(
        functools.partial(_ret_kernel, tb=tb),
        grid=(bsz, nt),
        in_specs=[col(c0), col(c0 + 1), col(c0 + 2), col(c0 + 3), rope, rope, tab, tab, tab,
                  pl.BlockSpec((RET_HEADS, 1, RET_HEAD_DIM), lambda b, i: (0, 0, 0)),
                  pl.BlockSpec((1, mw), lambda b, i: (0, 0))],
        out_specs=pl.BlockSpec((tb, mw), lambda b, i: (b * nt + i, 0)),
        out_shape=jax.ShapeDtypeStruct((bsz * seq, mw), F32),
        scratch_shapes=[pltpu.VMEM((RET_HEADS, RET_HEAD_DIM, RET_HEAD_DIM), F32)],
        compiler_params=_params(("arbitrary", "arbitrary")),
        name="ret",
    )(pg, pg, pg, pg, cosf, sinf, *tables, gn_g)


def _rwkv_kernel(r_ref, k_ref, v_ref, lo_ref, mur_ref, muk_ref, muv_ref, mulo_ref, w0_ref, a0_ref,
                 wl_ref, al_ref, gl_ref, kk_ref, ka_ref, rk_ref, gn_ref, tri_ref, ones_ref,
                 o_ref, cr, ck, cv, clo, state, s_r, s_kh, s_kt, s_b, s_v, s_lw, s_g, s_bonus, s_y, *, tb):
    @pl.when(pl.program_id(1) == 0)
    def _():
        for ref in (cr, ck, cv, clo, state):
            ref[...] = jnp.zeros_like(ref)

    def shift(ref, carry, mu_ref):
        p = ref[...]
        row = lax.broadcasted_iota(jnp.int32, p.shape, 0)
        prev = jnp.where(row == 0, carry[SUBLANES - 1:SUBLANES, :], pltpu.roll(p, 1, 0))
        carry[...] = ref[tb - SUBLANES:tb, :]
        return p + mu_ref[...] * (prev - p)

    r = shift(r_ref, cr, mur_ref)
    k = shift(k_ref, ck, muk_ref)
    v = shift(v_ref, cv, muv_ref)
    lo = shift(lo_ref, clo, mulo_ref)
    wd = lo[:, 0:DECAY_LORA]
    ad = lo[:, DECAY_LORA:DECAY_LORA + AAA_LORA]
    gd = lo[:, DECAY_LORA + AAA_LORA:LORA_COLS]
    logw = -math.exp(-0.5) * jax.nn.sigmoid(w0_ref[...] + _mm(jnp.tanh(wd).astype(BF16), wl_ref[...]))
    a = jax.nn.sigmoid(a0_ref[...] + _mm(ad.astype(BF16), al_ref[...]))
    s_g[...] = _mm(jax.nn.sigmoid(gd).astype(BF16), gl_ref[...])
    kk = k * kk_ref[...]
    kh = kk * lax.rsqrt(jnp.maximum(_mm_hi(kk * kk, ones_ref[...]), 1e-12))
    kt = k * (1.0 + (a - 1.0) * ka_ref[...])
    s_bonus[...] = _mm_hi(r * kt * rk_ref[...], ones_ref[...]) * v
    s_r[...] = r
    s_kh[...] = kh
    s_kt[...] = kt
    s_b[...] = a * kh
    s_v[...] = v
    s_lw[...] = logw

    ch = RWKV_CHUNK
    hd = RWKV_HEAD_DIM
    lane = lax.broadcasted_iota(jnp.int32, (ch, LANES), 1)
    rowi = lax.broadcasted_iota(jnp.int32, (ch, LANES), 0)
    first = lane < hd
    strict = (rowi > (lane % hd)).astype(F32)
    incl = (rowi >= (lane % hd)).astype(F32)
    lane2 = lax.broadcasted_iota(jnp.int32, (2 * ch, LANES), 1)
    row2 = lax.broadcasted_iota(jnp.int32, (2 * ch, LANES), 0)
    first2 = lane2 < hd
    blockdiag = ((row2 < hd) == (lane2 < hd)).astype(F32)

    def bd(x, swap):
        top = jnp.where(first, 0.0, x) if swap else jnp.where(first, x, 0.0)
        bot = jnp.where(first, x, 0.0) if swap else jnp.where(first, 0.0, x)
        return jnp.concatenate([top, bot], axis=0).astype(BF16)

    def chunk(c, carry):
        rs = pl.ds(pl.multiple_of(c * ch, ch), ch)
        lw = s_lw[rs, :]
        cum = _mm_hi(tri_ref[...], lw)
        cend = cum[ch - 1:ch, :]
        e_incl = jnp.exp(cum)
        e_excl = jnp.exp(cum - lw)
        e_inv = jnp.exp(-cum)
        e_end = jnp.exp(cend - cum)
        w_end = jnp.exp(cend)
        r_t = s_r[rs, :] * e_incl
        a_t = s_kh[rs, :] * e_excl
        k_t = s_kt[rs, :] * e_inv
        b_t = s_b[rs, :] * e_inv
        k_w = s_kt[rs, :] * e_end
        b_w = s_b[rs, :] * e_end
        vv = s_v[rs, :]
        for p in range(RWKV_PAIRS):
            cs = slice(p * LANES, (p + 1) * LANES)
            ar = jnp.concatenate([a_t[:, cs], r_t[:, cs]], axis=0)
            kb = jnp.concatenate([k_t[:, cs], b_t[:, cs]], axis=0).astype(BF16)
            bk = jnp.concatenate([b_t[:, cs], k_t[:, cs]], axis=0).astype(BF16)
            m_a = _nt(jnp.where(first2, ar, 0.0).astype(BF16), kb)
            m_b = _nt(jnp.where(first2, 0.0, ar).astype(BF16), bk)
            k_sc = jnp.where(first2, m_a, m_b)
            b_sc = jnp.where(first2, m_b, m_a)
            a_ak = k_sc[0:ch] * strict
            n_ab = b_sc[0:ch] * strict
            a_rk = k_sc[ch:2 * ch] * incl
            a_rb = b_sc[ch:2 * ch] * incl
            st = state[p]
            m_s = _nt(ar.astype(BF16), st.astype(BF16))
            v2 = vv[:, cs]
            v_bd = bd(v2, False)
            u = m_s[0:ch] + _mm(a_ak.astype(BF16), v_bd)
            u = u - _mm(n_ab.astype(BF16), bd(u, True))
            pw = n_ab
            for _ in range(5):
                pw = _mm(pw.astype(BF16), bd(pw, False))
                u = u + _mm(pw.astype(BF16), bd(u, True))
            y = m_s[ch:2 * ch] + _mm(jnp.concatenate([a_rk, -a_rb], axis=1).astype(BF16),
                                     jnp.concatenate([v_bd, bd(u, True)], axis=0))
            s_y[rs, cs] = y
            vu = jnp.concatenate([v2, u], axis=0).astype(BF16)
            kbw = jnp.concatenate([k_w[:, cs], -b_w[:, cs]], axis=0).astype(BF16)
            state[p] = st * w_end[:, cs] + _tn(vu, kbw) * blockdiag
        return carry

    lax.fori_loop(0, tb // ch, chunk, 0)

    y = s_y[...]
    inv_n = 1.0 / hd
    mu = _mm_hi(y, ones_ref[...]) * inv_n
    dev = y - mu
    var = _mm_hi(dev * dev, ones_ref[...]) * inv_n
    yn = dev * lax.rsqrt(var + RWKV_NORM_EPS) * gn_ref[...]
    o_ref[...] = (yn + s_bonus[...]) * s_g[...]


def _rwkv(pg, mu, w0, w_lora, a0, a_lora, g_lora, k_k, k_a, r_k, gn_g, bsz, seq):
    tb = min(RWKV_TB, seq)
    nt = seq // tb
    mw = MIX_WIDTH
    col = lambda cb: pl.BlockSpec((tb, mw), lambda b, i, cb=cb: (b * nt + i, cb))
    c0 = COL_RWKV // mw
    const = lambda shape: pl.BlockSpec(shape, lambda b, i: (0,) * len(shape))
    row = lambda t: t.reshape(1, -1)
    idx = jnp.arange(RWKV_CHUNK)
    tri = (idx[:, None] >= idx[None, :]).astype(F32)
    hid = jnp.arange(mw) // RWKV_HEAD_DIM
    ones = (hid[:, None] == hid[None, :]).astype(F32)
    vec = const((1, mw))
    buf = pltpu.VMEM((tb, mw), F32)
    return pl.pallas_call(
        functools.partial(_rwkv_kernel, tb=tb),
        grid=(bsz, nt),
        in_specs=[col(c0), col(c0 + 1), col(c0 + 2),
                  pl.BlockSpec((tb, LORA_COLS), lambda b, i: (b * nt + i, COL_LORA // LORA_COLS)),
                  vec, vec, vec, const((1, LORA_COLS)), vec, vec,
                  const((DECAY_LORA, mw)), const((AAA_LORA, mw)), const((GATE_LORA, mw)),
                  vec, vec, vec, vec, const((RWKV_CHUNK, RWKV_CHUNK)), const((mw, mw))],
        out_specs=pl.BlockSpec((tb, mw), lambda b, i: (b * nt + i, 0)),
        out_shape=jax.ShapeDtypeStruct((bsz * seq, mw), F32),
        scratch_shapes=[pltpu.VMEM((SUBLANES, mw), F32)] * 3 + [pltpu.VMEM((SUBLANES, LORA_COLS), F32),
                        pltpu.VMEM((RWKV_PAIRS, LANES, LANES), F32)] + [buf] * 9,
        compiler_params=_params(("arbitrary", "arbitrary")),
        name="rwkv",
    )(pg, pg, pg, pg, row(mu[0:mw]), row(mu[mw:2 * mw]), row(mu[2 * mw:3 * mw]), row(mu[3 * mw:]),
      row(w0), row(a0), w_lora.astype(BF16), a_lora.astype(BF16), g_lora.astype(BF16),
      row(k_k), row(k_a), row(r_k), row(gn_g), tri, ones)


def _merge_kernel(ch_ref, cb_ref, cc_ref, g0_ref, g1_ref, g2_ref, yret_ref, yrwkv_ref, x_ref, convw_ref,
                  wb_ref, wo_ref, gate1_ref, n2g_ref, shift2_ref, scale2_ref, wr_ref, br_ref, tri_ref,
                  x1_ref, h2_ref, route_ref, cnt_ref, ucarry, cnt_scr, *, tm):
    first_step = (pl.program_id(0) == 0) & (pl.program_id(1) == 0)

    @pl.when(first_step)
    def _():
        cnt_scr[...] = jnp.zeros_like(cnt_scr)

    @pl.when(pl.program_id(1) == 0)
    def _():
        ucarry[...] = jnp.zeros_like(ucarry)

    u = cc_ref[...] * ch_ref[...]
    row = lax.broadcasted_iota(jnp.int32, u.shape, 0)
    last = ucarry[SUBLANES - 1:SUBLANES, :]
    u1 = jnp.where(row == 0, last, pltpu.roll(u, 1, 0))
    u2 = jnp.where(row == 0, ucarry[SUBLANES - 2:SUBLANES - 1, :], jnp.where(row == 1, last, pltpu.roll(u, 2, 0)))
    ucarry[...] = u[tm - SUBLANES:tm, :]
    y_conv = cb_ref[...] * (convw_ref[0:1, :] * u2 + convw_ref[1:2, :] * u1 + convw_ref[2:3, :] * u)

    merged = g0_ref[...] * _mm(yret_ref[...].astype(BF16), wb_ref[0])
    merged = merged + g1_ref[...] * _mm(yrwkv_ref[...].astype(BF16), wb_ref[1])
    merged = merged + g2_ref[...] * _mm(y_conv.astype(BF16), wb_ref[2])
    x1 = x_ref[...] + gate1_ref[0] * _mm(merged.astype(BF16), wo_ref[...])
    x1_ref[...] = x1
    h2 = _norm_mod(x1, n2g_ref[...], shift2_ref[0], scale2_ref[0])
    h2_ref[...] = h2

    logits = _mm_hi(h2, wr_ref[...]) + br_ref[...]
    lane = lax.broadcasted_iota(jnp.int32, logits.shape, 1)
    lanef = lane.astype(F32)
    neg = -jnp.inf
    big = 1e9
    is_group = lane < N_GROUPS
    gl = jnp.where(is_group, logits, neg)
    gmax = jnp.max(gl, axis=-1, keepdims=True)
    gsel = jnp.min(jnp.where(gl == gmax, lanef, big), axis=-1, keepdims=True)
    p_group = 1.0 / jnp.sum(jnp.where(is_group, jnp.exp(gl - gmax), 0.0), axis=-1, keepdims=True)
    lo = ROUTE_LANE0 + EXPERTS_PER_GROUP * gsel
    el = jnp.where((lanef >= lo) & (lanef < lo + EXPERTS_PER_GROUP), logits, neg)
    m1 = jnp.max(el, axis=-1, keepdims=True)
    i1 = jnp.min(jnp.where(el == m1, lanef, big), axis=-1, keepdims=True)
    el2 = jnp.where(lanef == i1, neg, el)
    m2 = jnp.max(el2, axis=-1, keepdims=True)
    i2 = jnp.min(jnp.where(el2 == m2, lanef, big), axis=-1, keepdims=True)
    t = jnp.exp(m2 - m1)
    w1 = p_group / (1.0 + t)
    w2 = p_group * t / (1.0 + t)

    hit1 = lanef == i1
    hit2 = lanef == i2
    onehot = jnp.where(hit1 | hit2, 1.0, 0.0)
    before = _mm(tri_ref[...], onehot.astype(BF16)) + cnt_scr[0:1, :]
    rank1 = jnp.sum(jnp.where(hit1, before, 0.0), axis=-1, keepdims=True)
    rank2 = jnp.sum(jnp.where(hit2, before, 0.0), axis=-1, keepdims=True)
    cnt_scr[...] = cnt_scr[...] + jnp.sum(onehot, axis=0, keepdims=True)
    cnt_ref[...] = cnt_scr[...]

    out = jnp.where(lane == 0, i1 - ROUTE_LANE0, 0.0)
    out = jnp.where(lane == 1, i2 - ROUTE_LANE0, out)
    out = jnp.where(lane == 2, w1, out)
    out = jnp.where(lane == 3, w2, out)
    out = jnp.where(lane == 4, rank1, out)
    out = jnp.where(lane == 5, rank2, out)
    route_ref[...] = out


def _merge(pg, y_ret, y_rwkv, x, conv_w, w_branch, w_o, gate1, norm2_g, shift2, scale2, w_route, b_route,
           bsz, seq):
    tm = min(MERGE_TM, seq)
    nt = seq // tm
    d = D_MODEL
    mw = MIX_WIDTH
    n_tok = bsz * seq
    tok = lambda width, cb=0: pl.BlockSpec((tm, width), lambda b, i, cb=cb: (b * nt + i, cb))
    const = lambda shape: pl.BlockSpec(shape, lambda b, i: (0,) * len(shape))
    per_b = pl.BlockSpec((1, 1, d), lambda b, i: (b, 0, 0))
    cc0 = COL_CONV // mw
    gc0 = COL_GATE // d
    idx = jnp.arange(tm)
    tri = (idx[:, None] > idx[None, :]).astype(BF16)
    convw = jnp.zeros((SUBLANES, mw), F32).at[0:conv_w.shape[0]].set(conv_w)
    return pl.pallas_call(
        functools.partial(_merge_kernel, tm=tm),
        grid=(bsz, nt),
        in_specs=[tok(mw, cc0), tok(mw, cc0 + 1), tok(mw, cc0 + 2),
                  tok(d, gc0), tok(d, gc0 + 1), tok(d, gc0 + 2),
                  tok(mw), tok(mw), tok(d), const((SUBLANES, mw)),
                  const((N_BRANCHES, mw, d)), const((d, d)), per_b, const((1, d)), per_b, per_b,
                  const((d, LANES)), const((1, LANES)), const((tm, tm))],
        out_specs=[tok(d), tok(d), tok(LANES), const((SUBLANES, LANES))],
        out_shape=[jax.ShapeDtypeStruct((n_tok, d), F32), jax.ShapeDtypeStruct((n_tok, d), F32),
                   jax.ShapeDtypeStruct((n_tok, LANES), F32), jax.ShapeDtypeStruct((SUBLANES, LANES), F32)],
        scratch_shapes=[pltpu.VMEM((SUBLANES, mw), F32), pltpu.VMEM((SUBLANES, LANES), F32)],
        compiler_params=_params(("arbitrary", "arbitrary")),
        name="merge",
    )(pg, pg, pg, pg, pg, pg, y_ret, y_rwkv, x, convw, w_branch, w_o, gate1, norm2_g, shift2, scale2,
      w_route, b_route, tri)


def _row_copy(src, src_row, dst, dst_row, sem):
    return pltpu.make_async_copy(src.at[pl.ds(src_row, 1), :], dst.at[pl.ds(dst_row, 1), :], sem)


def _dispatch_kernel(dest_ref, h2_ref, rows_in_ref, rows_ref, sem, *, tile):
    del rows_in_ref

    def start(r, carry):
        for k in range(2):
            _row_copy(h2_ref, r, rows_ref, dest_ref[0, 0, 2 * r + k], sem).start()
        return carry

    def wait(r, carry):
        for k in range(2):
            _row_copy(h2_ref, 0, rows_ref, 0, sem).wait()
        return carry

    lax.fori_loop(0, tile, start, 0)
    lax.fori_loop(0, tile, wait, 0)


def _dispatch(dest, h2, n_rows):
    n_tok, d = h2.shape
    tile = min(ROW_TILE, n_tok)
    nt = n_tok // tile
    zeros = jnp.zeros((n_rows, d), h2.dtype)
    return pl.pallas_call(
        functools.partial(_dispatch_kernel, tile=tile),
        grid=(nt,),
        in_specs=[pl.BlockSpec((1, 1, 2 * tile), lambda i: (i, 0, 0), memory_space=pltpu.SMEM),
                  pl.BlockSpec((tile, d), lambda i: (i, 0)),
                  pl.BlockSpec(memory_space=pl.ANY)],
        out_specs=pl.BlockSpec(memory_space=pl.ANY),
        out_shape=jax.ShapeDtypeStruct((n_rows, d), h2.dtype),
        scratch_shapes=[pltpu.SemaphoreType.DMA(())],
        input_output_aliases={2: 0},
        compiler_params=_params(("arbitrary",)),
        name="dispatch",
    )(dest.reshape(nt, 1, 2 * tile), h2, zeros)


def _expert_kernel(be_ref, nused_ref, x_ref, wg_ref, wu_ref, wd_ref, y_ref):
    used = pl.program_id(0) < nused_ref[0]

    @pl.when(used)
    def _():
        xb = x_ref[...].astype(BF16)
        g = _mm(xb, wg_ref[0])
        u = _mm(xb, wu_ref[0])
        y_ref[...] = _mm(((g * jax.nn.sigmoid(g)) * u).astype(BF16), wd_ref[0])

    @pl.when(jnp.logical_not(used))
    def _():
        y_ref[...] = jnp.zeros_like(y_ref)


def _experts(block_e, n_used, x_rows, w_gate, w_up, w_down):
    n_rows, d = x_rows.shape
    ff = w_gate.shape[-1]
    grid_spec = pltpu.PrefetchScalarGridSpec(
        num_scalar_prefetch=2,
        grid=(n_rows // EXPERT_ROWS,),
        in_specs=[pl.BlockSpec((EXPERT_ROWS, d), lambda b, be, nu: (b, 0)),
                  pl.BlockSpec((1, d, ff), lambda b, be, nu: (be[b], 0, 0)),
                  pl.BlockSpec((1, d, ff), lambda b, be, nu: (be[b], 0, 0)),
                  pl.BlockSpec((1, ff, d), lambda b, be, nu: (be[b], 0, 0))],
        out_specs=pl.BlockSpec((EXPERT_ROWS, d), lambda b, be, nu: (b, 0)),
    )
    return pl.pallas_call(
        _expert_kernel,
        grid_spec=grid_spec,
        out_shape=jax.ShapeDtypeStruct((n_rows, d), F32),
        compiler_params=_params(("arbitrary",)),
        name="expert",
    )(block_e, n_used, x_rows, w_gate, w_up, w_down)


def _combine_kernel(dest_ref, x1_ref, route_ref, gate2_ref, fg_ref, rows_ref, o_ref, ybuf, sem, *, tile, final):
    def start(r, carry):
        for k in range(2):
            _row_copy(rows_ref, dest_ref[0, 0, 2 * r + k], ybuf.at[k], r, sem).start()
        return carry

    def wait(r, carry):
        for k in range(2):
            _row_copy(rows_ref, 0, ybuf.at[k], 0, sem).wait()
        return carry

    lax.fori_loop(0, tile, start, 0)
    lax.fori_loop(0, tile, wait, 0)
    route = route_ref[...]
    y = route[:, 2:3] * ybuf[0] + route[:, 3:4] * ybuf[1]
    x2 = x1_ref[...] + gate2_ref[0] * y
    if final:
        ms = jnp.mean(x2 * x2, axis=-1, keepdims=True)
        x2 = x2 * lax.rsqrt(ms + NORM_EPS) * fg_ref[...]
    o_ref[...] = x2


def _combine(dest, x1, route, gate2, final_g, y_rows, seq, final):
    n_tok, d = x1.shape
    tile = min(ROW_TILE, seq)
    nt = n_tok // tile
    per_seq = seq // tile
    return pl.pallas_call(
        functools.partial(_combine_kernel, tile=tile, final=final),
        grid=(nt,),
        in_specs=[pl.BlockSpec((1, 1, 2 * tile), lambda i: (i, 0, 0), memory_space=pltpu.SMEM),
                  pl.BlockSpec((tile, d), lambda i: (i, 0)),
                  pl.BlockSpec((tile, LANES), lambda i: (i, 0)),
                  pl.BlockSpec((1, 1, d), lambda i: (i // per_seq, 0, 0)),
                  pl.BlockSpec((1, d), lambda i: (0, 0)),
                  pl.BlockSpec(memory_space=pl.ANY)],
        out_specs=pl.BlockSpec((tile, d), lambda i: (i, 0)),
        out_shape=jax.ShapeDtypeStruct((n_tok, d), F32),
        scratch_shapes=[pltpu.VMEM((2, tile, d), F32), pltpu.SemaphoreType.DMA(())],
        compiler_params=_params(("arbitrary",)),
        name="combine",
    )(dest.reshape(nt, 1, 2 * tile), x1, route, gate2, final_g, y_rows)


def _moe(x1, h2, route, counts, gate2, final_g, w_gate, w_up, w_down, seq, final):
    n_tok = x1.shape[0]
    n_rows = 2 * n_tok + N_EXPERTS * EXPERT_ROWS
    n_blocks = n_rows // EXPERT_ROWS
    cnt = counts[0, ROUTE_LANE0:ROUTE_LANE0 + N_EXPERTS].astype(jnp.int32)
    padded = (cnt + EXPERT_ROWS - 1) // EXPERT_ROWS * EXPERT_ROWS
    pends = jnp.cumsum(padded)
    pstarts = pends - padded
    eid = route[:, 0:2].astype(jnp.int32)
    dest = (pstarts[eid] + route[:, 4:6].astype(jnp.int32)).reshape(-1)
    n_used = (pends[-1] // EXPERT_ROWS).astype(jnp.int32)
    blk = jnp.minimum(jnp.arange(n_blocks, dtype=jnp.int32), n_used - 1)
    block_e = jnp.minimum(jnp.searchsorted(pends, blk * EXPERT_ROWS, side='right'), N_EXPERTS - 1).astype(jnp.int32)
    x_rows = _dispatch(dest, h2, n_rows)
    y_rows = _experts(block_e, n_used.reshape(1), x_rows, w_gate, w_up, w_down)
    return _combine(dest, x1, route, gate2, final_g, y_rows, seq, final)


def kernel(x, c, positions, norm1_g, norm2_g, final_norm_g, w_ada, b_ada, w_in, w_gate, b_gate, ret_gn_g, rwkv_mu, rwkv_w0, rwkv_w_lora, rwkv_a0, rwkv_a_lora, rwkv_g_lora, rwkv_k_k, rwkv_k_a, rwkv_r_k, rwkv_gn_g, conv_w, w_branch, w_o, w_router_group, b_router_group, w_router_expert, b_router_expert, w_exp_gate, w_exp_up, w_exp_down):
    bsz, seq, d = x.shape
    assert d == D_MODEL and seq % RET_TB == 0
    n_layers = w_in.shape[0]
    n_tok = bsz * seq
    xt = x.reshape(n_tok, d)
    mod = _ada(c, w_ada, b_ada)
    cosf, sinf = _rope_tables(positions)
    ret_tables = _ret_tables()
    for l in range(n_layers):
        shift1, scale1, gate1, shift2, scale2, gate2 = [
            mod[l, :, i * d:(i + 1) * d].reshape(bsz, 1, d) for i in range(6)]
        o_rwkv = RET_COLS
        o_conv = RET_COLS + RWKV_COLS
        w_cat = jnp.concatenate([w_gate[l], w_in[l][:, :RET_COLS], w_in[l][:, o_conv:], w_in[l][:, o_rwkv:o_conv]],
                                axis=1).astype(BF16)
        b_cat = jnp.concatenate([b_gate[l], jnp.zeros((IN_COLS,), F32)])[None, :]
        pg = _proj(xt, norm1_g[l][None, :], shift1, scale1, w_cat, b_cat, bsz, seq)
        y_ret = _retention(pg, cosf, sinf, ret_tables, ret_gn_g[l][None, :], bsz, seq)
        y_rwkv = _rwkv(pg, rwkv_mu[l], rwkv_w0[l], rwkv_w_lora[l], rwkv_a0[l], rwkv_a_lora[l], rwkv_g_lora[l],
                       rwkv_k_k[l], rwkv_k_a[l], rwkv_r_k[l].reshape(-1), rwkv_gn_g[l], bsz, seq)
        w_route = jnp.zeros((d, LANES), F32)
        w_route = w_route.at[:, 0:N_GROUPS].set(w_router_group[l])
        w_route = w_route.at[:, ROUTE_LANE0:ROUTE_LANE0 + N_EXPERTS].set(w_router_expert[l])
        b_route = jnp.zeros((1, LANES), F32)
        b_route = b_route.at[0, 0:N_GROUPS].set(b_router_group[l])
        b_route = b_route.at[0, ROUTE_LANE0:ROUTE_LANE0 + N_EXPERTS].set(b_router_expert[l])
        x1, h2, route, counts = _merge(pg, y_ret, y_rwkv, xt, conv_w[l], w_branch[l].astype(BF16),
                                       w_o[l].astype(BF16), gate1, norm2_g[l][None, :], shift2, scale2,
                                       w_route, b_route, bsz, seq)
        xt = _moe(x1, h2, route, counts, gate2, final_norm_g[None, :], w_exp_gate[l].astype(BF16),
                  w_exp_up[l].astype(BF16), w_exp_down[l].astype(BF16), seq, final=(l == n_layers - 1))
    return xt.reshape(bsz, seq, d)
```

```python
import functools
import math

import jax
import jax.numpy as jnp
from jax import lax
from jax.experimental import pallas as pl
from jax.experimental.pallas import tpu as pltpu

F32 = jnp.float32
BF16 = jnp.bfloat16
HI = lax.Precision.HIGHEST

D_MODEL = 1024
MIX_WIDTH = D_MODEL // 2
RET_HEAD_DIM = 128
RET_HEADS = MIX_WIDTH // RET_HEAD_DIM
RET_CHUNK = 128
ROPE_BASE = 10000.0
RWKV_HEAD_DIM = 64
RWKV_HEADS = MIX_WIDTH // RWKV_HEAD_DIM
RWKV_PAIRS = RWKV_HEADS // 2
RWKV_CHUNK = 64
DECAY_LORA = 64
AAA_LORA = 64
GATE_LORA = 128
LORA_COLS = DECAY_LORA + AAA_LORA + GATE_LORA
N_BRANCHES = 3
RET_COLS = 4 * MIX_WIDTH
CONV_COLS = 3 * MIX_WIDTH
RWKV_COLS = 3 * MIX_WIDTH + LORA_COLS
IN_COLS = RET_COLS + RWKV_COLS + CONV_COLS
GATE_COLS = N_BRANCHES * D_MODEL
N_GROUPS = 4
EXPERTS_PER_GROUP = 8
N_EXPERTS = N_GROUPS * EXPERTS_PER_GROUP
EXPERT_FF = D_MODEL // 2
NORM_EPS = 1e-6
HEAD_NORM_EPS = 1e-5
RWKV_NORM_EPS = 64e-5

LANES = 128
SUBLANES = 8
VMEM_LIMIT = 56 * 1024 * 1024

COL_GATE = 0
COL_RET = GATE_COLS
COL_CONV = COL_RET + RET_COLS
COL_RWKV = COL_CONV + CONV_COLS
COL_LORA = COL_RWKV + 3 * MIX_WIDTH
PROJ_COLS = IN_COLS + GATE_COLS

PROJ_TM = 1024
PROJ_TN = 768
RET_TB = 512
RWKV_TB = 256
MERGE_TM = 256
ROW_TILE = 256
EXPERT_ROWS = 256
ROUTE_LANE0 = N_GROUPS


def _params(sem, vmem=VMEM_LIMIT):
    return pltpu.CompilerParams(dimension_semantics=sem, vmem_limit_bytes=vmem)


def _nt(a, b):
    return lax.dot_general(a, b, (((1,), (1,)), ((), ())), preferred_element_type=F32)


def _tn(a, b):
    return lax.dot_general(a, b, (((0,), (0,)), ((), ())), preferred_element_type=F32)


def _mm(a, b):
    return jnp.dot(a, b, preferred_element_type=F32)


def _mm_hi(a, b):
    return jnp.dot(a, b, precision=HI, preferred_element_type=F32)


def _mm_split(a, b01):
    hi = a.astype(BF16)
    lo = (a - hi.astype(F32)).astype(BF16)
    return _mm(hi, b01) + _mm(lo, b01)


def _mm_split_left(a01, b):
    hi = b.astype(BF16)
    lo = (b - hi.astype(F32)).astype(BF16)
    return _mm(a01, hi) + _mm(a01, lo)


def _ada_kernel(c_ref, w_ref, b_ref, o_ref):
    c = c_ref[...]
    s = c * jax.nn.sigmoid(c)
    o_ref[0] = _mm_hi(s, w_ref[0]) + b_ref[0]


def _ada(c, w_ada, b_ada):
    n_l, d, d6 = w_ada.shape
    bsz = c.shape[0]
    return pl.pallas_call(
        _ada_kernel,
        grid=(n_l, d6 // d),
        in_specs=[pl.BlockSpec((bsz, d), lambda l, j: (0, 0)),
                  pl.BlockSpec((1, d, d), lambda l, j: (l, 0, j)),
                  pl.BlockSpec((1, 1, d), lambda l, j: (l, 0, j))],
        out_specs=pl.BlockSpec((1, bsz, d), lambda l, j: (l, 0, j)),
        out_shape=jax.ShapeDtypeStruct((n_l, bsz, d6), F32),
        compiler_params=_params(("arbitrary", "arbitrary")),
        name="ada",
    )(c, w_ada, b_ada.reshape(n_l, 1, d6))


def _rope_kernel(pos_ref, invf_ref, sign_ref, cos_ref, sin_ref):
    ang = pos_ref[...] * invf_ref[...]
    cos_ref[...] = jnp.cos(ang)
    sin_ref[...] = jnp.sin(ang) * sign_ref[...]


def _rope_tables(positions):
    n_tok = positions.size
    half = RET_HEAD_DIM // 2
    inv_freq = ROPE_BASE ** (-jnp.arange(half, dtype=F32) / half)
    invf = jnp.concatenate([inv_freq, inv_freq])[None, :]
    sign = jnp.concatenate([-jnp.ones((half,), F32), jnp.ones((half,), F32)])[None, :]
    pos = jnp.broadcast_to(positions.astype(F32).reshape(n_tok, 1), (n_tok, RET_HEAD_DIM))
    tm = 1024
    row = pl.BlockSpec((tm, RET_HEAD_DIM), lambda i: (i, 0))
    const = pl.BlockSpec((1, RET_HEAD_DIM), lambda i: (0, 0))
    return pl.pallas_call(
        _rope_kernel,
        grid=(n_tok // tm,),
        in_specs=[row, const, const],
        out_specs=[row, row],
        out_shape=[jax.ShapeDtypeStruct((n_tok, RET_HEAD_DIM), F32)] * 2,
        compiler_params=_params(("arbitrary",)),
        name="rope",
    )(pos, invf, sign)


def _norm_mod(x, g, shift, scale):
    ms = jnp.mean(x * x, axis=-1, keepdims=True)
    return (x * lax.rsqrt(ms + NORM_EPS) * g) * (1.0 + scale) + shift


def _proj_kernel(x_ref, g_ref, shift_ref, scale_ref, w_ref, b_ref, o_ref, h_ref, *, n_gate):
    j = pl.program_id(2)

    @pl.when(j == 0)
    def _():
        h_ref[...] = _norm_mod(x_ref[...], g_ref[...], shift_ref[0], scale_ref[0]).astype(BF16)

    acc = _mm(h_ref[...], w_ref[...]) + b_ref[...]

    @pl.when(j < n_gate)
    def _():
        o_ref[...] = jax.nn.sigmoid(acc).astype(o_ref.dtype)

    @pl.when(j >= n_gate)
    def _():
        o_ref[...] = acc.astype(o_ref.dtype)


def _proj(x, norm_g, shift, scale, w_cat, b_cat, bsz, seq):
    d = x.shape[-1]
    tm = min(PROJ_TM, seq)
    nt = seq // tm
    return pl.pallas_call(
        functools.partial(_proj_kernel, n_gate=GATE_COLS // PROJ_TN),
        grid=(bsz, nt, PROJ_COLS // PROJ_TN),
        in_specs=[pl.BlockSpec((tm, d), lambda b, i, j: (b * nt + i, 0)),
                  pl.BlockSpec((1, d), lambda b, i, j: (0, 0)),
                  pl.BlockSpec((1, 1, d), lambda b, i, j: (b, 0, 0)),
                  pl.BlockSpec((1, 1, d), lambda b, i, j: (b, 0, 0)),
                  pl.BlockSpec((d, PROJ_TN), lambda b, i, j: (0, j)),
                  pl.BlockSpec((1, PROJ_TN), lambda b, i, j: (0, j))],
        out_specs=pl.BlockSpec((tm, PROJ_TN), lambda b, i, j: (b * nt + i, j)),
        out_shape=jax.ShapeDtypeStruct((bsz * seq, PROJ_COLS), BF16),
        scratch_shapes=[pltpu.VMEM((tm, d), BF16)],
        compiler_params=_params(("arbitrary", "arbitrary", "arbitrary")),
        name="proj",
    )(x, norm_g, shift, scale, w_cat, b_cat)


def _ret_tables():
    log_gamma = jnp.log1p(-jnp.exp2(-5.0 - jnp.arange(RET_HEADS, dtype=F32)))
    pos = jnp.arange(RET_CHUNK, dtype=F32)
    rel = pos[:, None] - pos[None, :]
    decay_in = jnp.where(rel >= 0, jnp.exp(log_gamma[:, None, None] * jnp.maximum(rel, 0.0)), 0.0)
    decay_k = jnp.exp(log_gamma[:, None] * (RET_CHUNK - 1 - pos))
    decay_q = jnp.exp(log_gamma[:, None] * (pos + 1.0))
    decay_chunk = jnp.exp(log_gamma * RET_CHUNK)
    bcast = lambda t: jnp.broadcast_to(t[:, :, None], (RET_HEADS, RET_CHUNK, RET_HEAD_DIM))
    dc = jnp.broadcast_to(decay_chunk[:, None, None], (RET_HEADS, 1, RET_HEAD_DIM))
    return decay_in, bcast(decay_k), bcast(decay_q), dc


def _ret_kernel(q_ref, k_ref, v_ref, g_ref, cos_ref, sin_ref, din_ref, zeta_ref, xi_ref, dc_ref, gn_ref,
                o_ref, state_ref, *, tb):
    @pl.when(pl.program_id(1) == 0)
    def _():
        state_ref[...] = jnp.zeros_like(state_ref)

    k_scale = RET_HEAD_DIM ** -0.5
    for c in range(tb // RET_CHUNK):
        rows = slice(c * RET_CHUNK, (c + 1) * RET_CHUNK)
        cosf = cos_ref[rows, :]
        sinf = sin_ref[rows, :]
        for h in range(RET_HEADS):
            cols = slice(h * RET_HEAD_DIM, (h + 1) * RET_HEAD_DIM)
            q = q_ref[rows, cols].astype(F32)
            k = k_ref[rows, cols].astype(F32)
            qr = q * cosf + pltpu.roll(q, RET_HEAD_DIM // 2, 1) * sinf
            kr = (k * cosf + pltpu.roll(k, RET_HEAD_DIM // 2, 1) * sinf) * k_scale
            qb = qr.astype(BF16)
            vb = v_ref[rows, cols].astype(BF16)
            scores = _nt(qb, kr.astype(BF16)) * din_ref[h]
            inner = _mm(scores.astype(BF16), vb)
            st = state_ref[h]
            cross = _mm(qb, st.astype(BF16)) * xi_ref[h]
            state_ref[h] = st * dc_ref[h] + _tn((kr * zeta_ref[h]).astype(BF16), vb)
            out = inner + cross
            mu = jnp.mean(out, axis=-1, keepdims=True)
            dev = out - mu
            var = jnp.mean(dev * dev, axis=-1, keepdims=True)
            g = g_ref[rows, cols].astype(F32)
            y = (g * jax.nn.sigmoid(g)) * (dev * lax.rsqrt(var + HEAD_NORM_EPS) * gn_ref[:, cols])
            o_ref[rows, cols] = y.astype(o_ref.dtype)


def _retention(pg, cosf, sinf, tables, gn_g, bsz, seq):
    tb = min(RET_TB, seq)
    nt = seq // tb
    mw = MIX_WIDTH
    col = lambda cb: pl.BlockSpec((tb, mw), lambda b, i, cb=cb: (b * nt + i, cb))
    rope = pl.BlockSpec((tb, RET_HEAD_DIM), lambda b, i: (b * nt + i, 0))
    tab = pl.BlockSpec((RET_HEADS, RET_CHUNK, RET_HEAD_DIM), lambda b, i: (0, 0, 0))
    c0 = COL_RET // mw
    return pl.pallas_call(
        functools.partial(_ret_kernel, tb=tb),
        grid=(bsz, nt),
        in_specs=[col(c0), col(c0 + 1), col(c0 + 2), col(c0 + 3), rope, rope, tab, tab, tab,
                  pl.BlockSpec((RET_HEADS, 1, RET_HEAD_DIM), lambda b, i: (0, 0, 0)),
                  pl.BlockSpec((1, mw), lambda b, i: (0, 0))],
        out_specs=pl.BlockSpec((tb, mw), lambda b, i: (b * nt + i, 0)),
        out_shape=jax.ShapeDtypeStruct((bsz * seq, mw), BF16),
        scratch_shapes=[pltpu.VMEM((RET_HEADS, RET_HEAD_DIM, RET_HEAD_DIM), F32)],
        compiler_params=_params(("arbitrary", "arbitrary")),
        name="ret",
    )(pg, pg, pg, pg, cosf, sinf, *tables, gn_g)


def _rwkv_kernel(r_ref, k_ref, v_ref, lo_ref, mur_ref, muk_ref, muv_ref, mulo_ref, w0_ref, a0_ref,
                 wl_ref, al_ref, gl_ref, kk_ref, ka_ref, rk_ref, gn_ref, tri_ref, ones_ref,
                 o_ref, cr, ck, cv, clo, state, s_rt, s_at, s_kt, s_bt, s_kw, s_bw, s_v, s_g, s_bonus, s_y,
                 *, tb):
    @pl.when(pl.program_id(1) == 0)
    def _():
        for ref in (cr, ck, cv, clo, state):
            ref[...] = jnp.zeros_like(ref)

    def shift(ref, carry, mu_ref):
        p = ref[...].astype(F32)
        row = lax.broadcasted_iota(jnp.int32, p.shape, 0)
        prev = jnp.where(row == 0, carry[SUBLANES - 1:SUBLANES, :], pltpu.roll(p, 1, 0))
        carry[...] = p[tb - SUBLANES:tb, :]
        return p + mu_ref[...] * (prev - p)

    r = shift(r_ref, cr, mur_ref)
    k = shift(k_ref, ck, muk_ref)
    v = shift(v_ref, cv, muv_ref)
    lo = shift(lo_ref, clo, mulo_ref)
    wd = lo[:, 0:DECAY_LORA]
    ad = lo[:, DECAY_LORA:DECAY_LORA + AAA_LORA]
    gd = lo[:, DECAY_LORA + AAA_LORA:LORA_COLS]
    logw = -math.exp(-0.5) * jax.nn.sigmoid(w0_ref[...] + _mm(jnp.tanh(wd).astype(BF16), wl_ref[...]))
    a = jax.nn.sigmoid(a0_ref[...] + _mm(ad.astype(BF16), al_ref[...]))
    s_g[...] = _mm(jax.nn.sigmoid(gd).astype(BF16), gl_ref[...])
    kk = k * kk_ref[...]
    kh = kk * lax.rsqrt(jnp.maximum(_mm((kk * kk).astype(BF16), ones_ref[...]), 1e-12))
    kt = k * (1.0 + (a - 1.0) * ka_ref[...])
    s_bonus[...] = _mm((r * kt * rk_ref[...]).astype(BF16), ones_ref[...]) * v
    ch = RWKV_CHUNK
    hd = RWKV_HEAD_DIM
    n_ch = tb // ch
    mw = MIX_WIDTH
    b = a * kh
    cum = _mm_split_left(tri_ref[...], logw)
    ends = [cum[(c + 1) * ch - 1:(c + 1) * ch, :] for c in range(n_ch)]
    cend = jnp.concatenate([jnp.broadcast_to(e, (ch, mw)) for e in ends], axis=0)
    e_inv = jnp.exp(-cum)
    e_end = jnp.exp(cend - cum)
    s_rt[...] = r * jnp.exp(cum)
    s_at[...] = kh * jnp.exp(cum - logw)
    s_kt[...] = kt * e_inv
    s_bt[...] = b * e_inv
    s_kw[...] = kt * e_end
    s_bw[...] = b * e_end
    s_v[...] = v

    lane = lax.broadcasted_iota(jnp.int32, (ch, LANES), 1)
    rowi = lax.broadcasted_iota(jnp.int32, (ch, LANES), 0)
    strict = (rowi > (lane % hd)).astype(F32)
    incl = (rowi >= (lane % hd)).astype(F32)
    lane2 = lax.broadcasted_iota(jnp.int32, (2 * ch, LANES), 1)
    row2 = lax.broadcasted_iota(jnp.int32, (2 * ch, LANES), 0)
    first2 = lane2 < hd
    blockdiag = ((row2 < hd) == (lane2 < hd)).astype(F32)
    zeros_sq = jnp.zeros((2 * ch, LANES), BF16)

    def bd(x, swap):
        lane_x = lax.broadcasted_iota(jnp.int32, x.shape, 1)
        first = (lane_x % LANES) < hd
        top = jnp.where(first, 0.0, x) if swap else jnp.where(first, x, 0.0)
        bot = jnp.where(first, x, 0.0) if swap else jnp.where(first, 0.0, x)
        return jnp.concatenate([top, bot], axis=0).astype(BF16)

    items = [(c, p) for c in range(n_ch) for p in range(RWKV_PAIRS)]
    rows = lambda c: slice(c * ch, (c + 1) * ch)
    cols = lambda p: slice(p * LANES, (p + 1) * LANES)
    blk = lambda ref, i: ref[rows(i[0]), cols(i[1])]
    a_t = [blk(s_at, i) for i in items]
    r_t = [blk(s_rt, i) for i in items]
    k_t = [blk(s_kt, i) for i in items]
    b_t = [blk(s_bt, i) for i in items]
    n_it = range(len(items))
    ar = [jnp.concatenate([a_t[i], r_t[i]], axis=0) for i in n_it]
    kb = [jnp.concatenate([k_t[i], b_t[i]], axis=0).astype(BF16) for i in n_it]
    bk = [jnp.concatenate([b_t[i], k_t[i]], axis=0).astype(BF16) for i in n_it]
    m_a = [_nt(jnp.where(first2, ar[i], 0.0).astype(BF16), kb[i]) for i in n_it]
    m_b = [_nt(jnp.where(first2, 0.0, ar[i]).astype(BF16), bk[i]) for i in n_it]
    k_sc = [jnp.where(first2, m_a[i], m_b[i]) for i in n_it]
    b_sc = [jnp.where(first2, m_b[i], m_a[i]) for i in n_it]
    a_ak = [k_sc[i][0:ch] * strict for i in n_it]
    n_ab = [b_sc[i][0:ch] * strict for i in n_it]
    a_rk = [k_sc[i][ch:2 * ch] * incl for i in n_it]
    a_rb = [b_sc[i][ch:2 * ch] * incl for i in n_it]
    v2 = [blk(s_v, i) for i in items]
    v_bd = [bd(v2[i], False) for i in n_it]
    x = [jnp.concatenate([_mm(a_ak[i].astype(BF16), v_bd[i]), a_t[i]], axis=1) for i in n_it]
    pw = n_ab
    for j in range(6):
        if j < 5:
            res = [_mm(pw[i].astype(BF16), jnp.concatenate([bd(x[i], True), bd(pw[i], False)], axis=1))
                   for i in n_it]
            pw_x = [res[i][:, 0:2 * LANES] for i in n_it]
            pw = [res[i][:, 2 * LANES:3 * LANES] for i in n_it]
        else:
            pw_x = [_mm(pw[i].astype(BF16), bd(x[i], True)) for i in n_it]
        x = [x[i] - pw_x[i] if j == 0 else x[i] + pw_x[i] for i in n_it]
    u_loc = [x[i][:, 0:LANES] for i in n_it]
    z = [x[i][:, LANES:2 * LANES] for i in n_it]
    yg = [_mm(jnp.concatenate([a_rk[i], -a_rb[i]], axis=1).astype(BF16),
              jnp.concatenate([jnp.concatenate([v_bd[i], zeros_sq], axis=1), bd(x[i], True)], axis=0))
          for i in n_it]
    y_loc = [yg[i][:, 0:LANES] for i in n_it]
    g = [(r_t[i] + yg[i][:, LANES:2 * LANES]).astype(BF16) for i in n_it]
    q = [_tn(jnp.concatenate([v2[i], u_loc[i]], axis=0).astype(BF16),
             jnp.concatenate([blk(s_kw, items[i]), -blk(s_bw, items[i])], axis=0).astype(BF16)) * blockdiag
         for i in n_it]
    pm = [(_tn(z[i].astype(BF16), blk(s_bw, items[i]).astype(BF16)) * blockdiag).astype(BF16) for i in n_it]

    for p in range(RWKV_PAIRS):
        st = state[p]
        for c in range(n_ch):
            i = c * RWKV_PAIRS + p
            sb = st.astype(BF16)
            s_y[rows(c), cols(p)] = y_loc[i] + _nt(g[i], sb)
            st = st * jnp.exp(ends[c][:, cols(p)]) - _mm(sb, pm[i]) + q[i]
        state[p] = st

    y = s_y[...]
    inv_n = 1.0 / hd
    mu = _mm_split(y, ones_ref[...]) * inv_n
    dev = y - mu
    var = _mm((dev * dev).astype(BF16), ones_ref[...]) * inv_n
    yn = dev * lax.rsqrt(var + RWKV_NORM_EPS) * gn_ref[...]
    o_ref[...] = ((yn + s_bonus[...]) * s_g[...]).astype(o_ref.dtype)


def _rwkv(pg, mu, w0, w_lora, a0, a_lora, g_lora, k_k, k_a, r_k, gn_g, bsz, seq):
    tb = min(RWKV_TB, seq)
    nt = seq // tb
    mw = MIX_WIDTH
    col = lambda cb: pl.BlockSpec((tb, mw), lambda b, i, cb=cb: (b * nt + i, cb))
    c0 = COL_RWKV // mw
    const = lambda shape: pl.BlockSpec(shape, lambda b, i: (0,) * len(shape))
    row = lambda t: t.reshape(1, -1)
    idx = jnp.arange(tb)
    same_chunk = (idx[:, None] // RWKV_CHUNK) == (idx[None, :] // RWKV_CHUNK)
    tri = ((idx[:, None] >= idx[None, :]) & same_chunk).astype(BF16)
    hid = jnp.arange(mw) // RWKV_HEAD_DIM
    ones = (hid[:, None] == hid[None, :]).astype(BF16)
    vec = const((1, mw))
    buf = pltpu.VMEM((tb, mw), F32)
    return pl.pallas_call(
        functools.partial(_rwkv_kernel, tb=tb),
        grid=(bsz, nt),
        in_specs=[col(c0), col(c0 + 1), col(c0 + 2),
                  pl.BlockSpec((tb, LORA_COLS), lambda b, i: (b * nt + i, COL_LORA // LORA_COLS)),
                  vec, vec, vec, const((1, LORA_COLS)), vec, vec,
                  const((DECAY_LORA, mw)), const((AAA_LORA, mw)), const((GATE_LORA, mw)),
                  vec, vec, vec, vec, const((tb, tb)), const((mw, mw))],
        out_specs=pl.BlockSpec((tb, mw), lambda b, i: (b * nt + i, 0)),
        out_shape=jax.ShapeDtypeStruct((bsz * seq, mw), BF16),
        scratch_shapes=[pltpu.VMEM((SUBLANES, mw), F32)] * 3 + [pltpu.VMEM((SUBLANES, LORA_COLS), F32),
                        pltpu.VMEM((RWKV_PAIRS, LANES, LANES), F32)] + [buf] * 10,
        compiler_params=_params(("arbitrary", "arbitrary")),
        name="rwkv",
    )(pg, pg, pg, pg, row(mu[0:mw]), row(mu[mw:2 * mw]), row(mu[2 * mw:3 * mw]), row(mu[3 * mw:]),
      row(w0), row(a0), w_lora.astype(BF16), a_lora.astype(BF16), g_lora.astype(BF16),
      row(k_k), row(k_a), row(r_k), row(gn_g), tri, ones)


def _merge_kernel(ch_ref, cb_ref, cc_ref, g0_ref, g1_ref, g2_ref, yret_ref, yrwkv_ref, x_ref, convw_ref,
                  wb_ref, wo_ref, gate1_ref, n2g_ref, shift2_ref, scale2_ref, wr_ref, br_ref, tri_ref,
                  x1_ref, h2_ref, route_ref, cnt_ref, ucarry, cnt_scr, *, tm):
    first_step = (pl.program_id(0) == 0) & (pl.program_id(1) == 0)

    @pl.when(first_step)
    def _():
        cnt_scr[...] = jnp.zeros_like(cnt_scr)

    @pl.when(pl.program_id(1) == 0)
    def _():
        ucarry[...] = jnp.zeros_like(ucarry)

    u = cc_ref[...].astype(F32) * ch_ref[...].astype(F32)
    row = lax.broadcasted_iota(jnp.int32, u.shape, 0)
    last = ucarry[SUBLANES - 1:SUBLANES, :]
    u1 = jnp.where(row == 0, last, pltpu.roll(u, 1, 0))
    u2 = jnp.where(row == 0, ucarry[SUBLANES - 2:SUBLANES - 1, :], jnp.where(row == 1, last, pltpu.roll(u, 2, 0)))
    ucarry[...] = u[tm - SUBLANES:tm, :]
    y_conv = cb_ref[...].astype(F32) * (convw_ref[0:1, :] * u2 + convw_ref[1:2, :] * u1 + convw_ref[2:3, :] * u)

    merged = g0_ref[...].astype(F32) * _mm(yret_ref[...], wb_ref[0])
    merged = merged + g1_ref[...].astype(F32) * _mm(yrwkv_ref[...], wb_ref[1])
    merged = merged + g2_ref[...].astype(F32) * _mm(y_conv.astype(BF16), wb_ref[2])
    x1 = x_ref[...] + gate1_ref[0] * _mm(merged.astype(BF16), wo_ref[...])
    x1_ref[...] = x1
    h2 = _norm_mod(x1, n2g_ref[...], shift2_ref[0], scale2_ref[0])
    h2_ref[...] = h2

    h_hi = h2.astype(BF16)
    h_lo = (h2 - h_hi.astype(F32)).astype(BF16)
    logits = _mm(h_hi, wr_ref[0]) + _mm(h_lo, wr_ref[0]) + _mm(h_hi, wr_ref[1]) + br_ref[...]
    lane = lax.broadcasted_iota(jnp.int32, logits.shape, 1)
    lanef = lane.astype(F32)
    neg = -jnp.inf
    big = 1e9
    is_group = lane < N_GROUPS
    gl = jnp.where(is_group, logits, neg)
    gmax = jnp.max(gl, axis=-1, keepdims=True)
    gsel = jnp.min(jnp.where(gl == gmax, lanef, big), axis=-1, keepdims=True)
    p_group = 1.0 / jnp.sum(jnp.where(is_group, jnp.exp(gl - gmax), 0.0), axis=-1, keepdims=True)
    lo = ROUTE_LANE0 + EXPERTS_PER_GROUP * gsel
    el = jnp.where((lanef >= lo) & (lanef < lo + EXPERTS_PER_GROUP), logits, neg)
    m1 = jnp.max(el, axis=-1, keepdims=True)
    i1 = jnp.min(jnp.where(el == m1, lanef, big), axis=-1, keepdims=True)
    el2 = jnp.where(lanef == i1, neg, el)
    m2 = jnp.max(el2, axis=-1, keepdims=True)
    i2 = jnp.min(jnp.where(el2 == m2, lanef, big), axis=-1, keepdims=True)
    t = jnp.exp(m2 - m1)
    w1 = p_group / (1.0 + t)
    w2 = p_group * t / (1.0 + t)

    hit1 = lanef == i1
    hit2 = lanef == i2
    onehot = jnp.where(hit1 | hit2, 1.0, 0.0)
    before = _mm(tri_ref[...], onehot.astype(BF16)) + cnt_scr[0:1, :]
    rank1 = jnp.sum(jnp.where(hit1, before, 0.0), axis=-1, keepdims=True)
    rank2 = jnp.sum(jnp.where(hit2, before, 0.0), axis=-1, keepdims=True)
    cnt_scr[...] = cnt_scr[...] + jnp.sum(onehot, axis=0, keepdims=True)
    cnt_ref[...] = cnt_scr[...]

    out = jnp.where(lane == 0, i1 - ROUTE_LANE0, 0.0)
    out = jnp.where(lane == 1, i2 - ROUTE_LANE0, out)
    out = jnp.where(lane == 2, w1, out)
    out = jnp.where(lane == 3, w2, out)
    out = jnp.where(lane == 4, rank1, out)
    out = jnp.where(lane == 5, rank2, out)
    route_ref[...] = out


def _merge(pg, y_ret, y_rwkv, x, conv_w, w_branch, w_o, gate1, norm2_g, shift2, scale2, w_route, b_route,
           bsz, seq):
    tm = min(MERGE_TM, seq)
    nt = seq // tm
    d = D_MODEL
    mw = MIX_WIDTH
    n_tok = bsz * seq
    tok = lambda width, cb=0: pl.BlockSpec((tm, width), lambda b, i, cb=cb: (b * nt + i, cb))
    const = lambda shape: pl.BlockSpec(shape, lambda b, i: (0,) * len(shape))
    per_b = pl.BlockSpec((1, 1, d), lambda b, i: (b, 0, 0))
    cc0 = COL_CONV // mw
    gc0 = COL_GATE // d
    idx = jnp.arange(tm)
    tri = (idx[:, None] > idx[None, :]).astype(BF16)
    convw = jnp.zeros((SUBLANES, mw), F32).at[0:conv_w.shape[0]].set(conv_w)
    return pl.pallas_call(
        functools.partial(_merge_kernel, tm=tm),
        grid=(bsz, nt),
        in_specs=[tok(mw, cc0), tok(mw, cc0 + 1), tok(mw, cc0 + 2),
                  tok(d, gc0), tok(d, gc0 + 1), tok(d, gc0 + 2),
                  tok(mw), tok(mw), tok(d), const((SUBLANES, mw)),
                  const((N_BRANCHES, mw, d)), const((d, d)), per_b, const((1, d)), per_b, per_b,
                  const((2, d, LANES)), const((1, LANES)), const((tm, tm))],
        out_specs=[tok(d), tok(d), tok(LANES), const((SUBLANES, LANES))],
        out_shape=[jax.ShapeDtypeStruct((n_tok, d), F32), jax.ShapeDtypeStruct((n_tok, d), F32),
                   jax.ShapeDtypeStruct((n_tok, LANES), F32), jax.ShapeDtypeStruct((SUBLANES, LANES), F32)],
        scratch_shapes=[pltpu.VMEM((SUBLANES, mw), F32), pltpu.VMEM((SUBLANES, LANES), F32)],
        compiler_params=_params(("arbitrary", "arbitrary")),
        name="merge",
    )(pg, pg, pg, pg, pg, pg, y_ret, y_rwkv, x, convw, w_branch, w_o, gate1, norm2_g, shift2, scale2,
      jnp.stack([w_route.astype(BF16), (w_route - w_route.astype(BF16).astype(F32)).astype(BF16)]), b_route, tri)


def _row_copy(src, src_row, dst, dst_row, sem):
    return pltpu.make_async_copy(src.at[pl.ds(src_row, 1), :], dst.at[pl.ds(dst_row, 1), :], sem)


def _dispatch_kernel(dest_ref, h2_ref, rows_in_ref, rows_ref, sem, *, tile):
    del rows_in_ref

    def start(r, carry):
        for k in range(2):
            _row_copy(h2_ref, r, rows_ref, dest_ref[0, 0, 2 * r + k], sem).start()
        return carry

    def wait(r, carry):
        for k in range(2):
            _row_copy(h2_ref, 0, rows_ref, 0, sem).wait()
        return carry

    lax.fori_loop(0, tile, start, 0, unroll=8)
    lax.fori_loop(0, tile, wait, 0, unroll=8)


def _dispatch(dest, h2, n_rows):
    n_tok, d = h2.shape
    tile = min(ROW_TILE, n_tok)
    nt = n_tok // tile
    zeros = jnp.zeros((n_rows, d), h2.dtype)
    return pl.pallas_call(
        functools.partial(_dispatch_kernel, tile=tile),
        grid=(nt,),
        in_specs=[pl.BlockSpec((1, 1, 2 * tile), lambda i: (i, 0, 0), memory_space=pltpu.SMEM),
                  pl.BlockSpec((tile, d), lambda i: (i, 0)),
                  pl.BlockSpec(memory_space=pl.ANY)],
        out_specs=pl.BlockSpec(memory_space=pl.ANY),
        out_shape=jax.ShapeDtypeStruct((n_rows, d), h2.dtype),
        scratch_shapes=[pltpu.SemaphoreType.DMA(())],
        input_output_aliases={2: 0},
        compiler_params=_params(("arbitrary",)),
        name="dispatch",
    )(dest.reshape(nt, 1, 2 * tile), h2, zeros)


def _expert_kernel(be_ref, nused_ref, x_ref, wg_ref, wu_ref, wd_ref, y_ref):
    used = pl.program_id(0) < nused_ref[0]

    @pl.when(used)
    def _():
        xb = x_ref[...].astype(BF16)
        g = _mm(xb, wg_ref[0])
        u = _mm(xb, wu_ref[0])
        y_ref[...] = _mm(((g * jax.nn.sigmoid(g)) * u).astype(BF16), wd_ref[0])

    @pl.when(jnp.logical_not(used))
    def _():
        y_ref[...] = jnp.zeros_like(y_ref)


def _experts(block_e, n_used, x_rows, w_gate, w_up, w_down):
    n_rows, d = x_rows.shape
    ff = w_gate.shape[-1]
    grid_spec = pltpu.PrefetchScalarGridSpec(
        num_scalar_prefetch=2,
        grid=(n_rows // EXPERT_ROWS,),
        in_specs=[pl.BlockSpec((EXPERT_ROWS, d), lambda b, be, nu: (b, 0)),
                  pl.BlockSpec((1, d, ff), lambda b, be, nu: (be[b], 0, 0)),
                  pl.BlockSpec((1, d, ff), lambda b, be, nu: (be[b], 0, 0)),
                  pl.BlockSpec((1, ff, d), lambda b, be, nu: (be[b], 0, 0))],
        out_specs=pl.BlockSpec((EXPERT_ROWS, d), lambda b, be, nu: (b, 0)),
    )
    return pl.pallas_call(
        _expert_kernel,
        grid_spec=grid_spec,
        out_shape=jax.ShapeDtypeStruct((n_rows, d), F32),
        compiler_params=_params(("arbitrary",)),
        name="expert",
    )(block_e, n_used, x_rows, w_gate, w_up, w_down)


def _combine_kernel(dest_ref, x1_ref, route_ref, gate2_ref, fg_ref, rows_ref, o_ref, ybuf, sem, *, tile, final):
    def start(r, carry):
        for k in range(2):
            _row_copy(rows_ref, dest_ref[0, 0, 2 * r + k], ybuf.at[k], r, sem).start()
        return carry

    def wait(r, carry):
        for k in range(2):
            _row_copy(rows_ref, 0, ybuf.at[k], 0, sem).wait()
        return carry

    lax.fori_loop(0, tile, start, 0, unroll=8)
    lax.fori_loop(0, tile, wait, 0, unroll=8)
    route = route_ref[...]
    y = route[:, 2:3] * ybuf[0] + route[:, 3:4] * ybuf[1]
    x2 = x1_ref[...] + gate2_ref[0] * y
    if final:
        ms = jnp.mean(x2 * x2, axis=-1, keepdims=True)
        x2 = x2 * lax.rsqrt(ms + NORM_EPS) * fg_ref[...]
    o_ref[...] = x2


def _combine(dest, x1, route, gate2, final_g, y_rows, seq, final):
    n_tok, d = x1.shape
    tile = min(ROW_TILE, seq)
    nt = n_tok // tile
    per_seq = seq // tile
    return pl.pallas_call(
        functools.partial(_combine_kernel, tile=tile, final=final),
        grid=(nt,),
        in_specs=[pl.BlockSpec((1, 1, 2 * tile), lambda i: (i, 0, 0), memory_space=pltpu.SMEM),
                  pl.BlockSpec((tile, d), lambda i: (i, 0)),
                  pl.BlockSpec((tile, LANES), lambda i: (i, 0)),
                  pl.BlockSpec((1, 1, d), lambda i: (i // per_seq, 0, 0)),
                  pl.BlockSpec((1, d), lambda i: (0, 0)),
                  pl.BlockSpec(memory_space=pl.ANY)],
        out_specs=pl.BlockSpec((tile, d), lambda i: (i, 0)),
        out_shape=jax.ShapeDtypeStruct((n_tok, d), F32),
        scratch_shapes=[pltpu.VMEM((2, tile, d), F32), pltpu.SemaphoreType.DMA(())],
        compiler_params=_params(("arbitrary",)),
        name="combine",
    )(dest.reshape(nt, 1, 2 * tile), x1, route, gate2, final_g, y_rows)


def _moe(x1, h2, route, counts, gate2, final_g, w_gate, w_up, w_down, seq, final):
    n_tok = x1.shape[0]
    n_rows = 2 * n_tok + N_EXPERTS * EXPERT_ROWS
    n_blocks = n_rows // EXPERT_ROWS
    cnt = counts[0, ROUTE_LANE0:ROUTE_LANE0 + N_EXPERTS].astype(jnp.int32)
    padded = (cnt + EXPERT_ROWS - 1) // EXPERT_ROWS * EXPERT_ROWS
    pends = jnp.cumsum(padded)
    pstarts = pends - padded
    eid = route[:, 0:2].astype(jnp.int32)
    dest = (pstarts[eid] + route[:, 4:6].astype(jnp.int32)).reshape(-1)
    n_used = (pends[-1] // EXPERT_ROWS).astype(jnp.int32)
    blk = jnp.minimum(jnp.arange(n_blocks, dtype=jnp.int32), n_used - 1)
    block_e = jnp.sum((pends[None, :] <= (blk * EXPERT_ROWS)[:, None]).astype(jnp.int32), axis=1)
    block_e = jnp.minimum(block_e, N_EXPERTS - 1)
    x_rows = _dispatch(dest, h2, n_rows)
    y_rows = _experts(block_e, n_used.reshape(1), x_rows, w_gate, w_up, w_down)
    return _combine(dest, x1, route, gate2, final_g, y_rows, seq, final)


def kernel(x, c, positions, norm1_g, norm2_g, final_norm_g, w_ada, b_ada, w_in, w_gate, b_gate, ret_gn_g, rwkv_mu, rwkv_w0, rwkv_w_lora, rwkv_a0, rwkv_a_lora, rwkv_g_lora, rwkv_k_k, rwkv_k_a, rwkv_r_k, rwkv_gn_g, conv_w, w_branch, w_o, w_router_group, b_router_group, w_router_expert, b_router_expert, w_exp_gate, w_exp_up, w_exp_down):
    bsz, seq, d = x.shape
    assert d == D_MODEL and seq % RET_TB == 0
    n_layers = w_in.shape[0]
    n_tok = bsz * seq
    xt = x.reshape(n_tok, d)
    mod = _ada(c, w_ada, b_ada)
    cosf, sinf = _rope_tables(positions)
    ret_tables = _ret_tables()
    for l in range(n_layers):
        shift1, scale1, gate1, shift2, scale2, gate2 = [
            mod[l, :, i * d:(i + 1) * d].reshape(bsz, 1, d) for i in range(6)]
        o_rwkv = RET_COLS
        o_conv = RET_COLS + RWKV_COLS
        w_cat = jnp.concatenate([w_gate[l], w_in[l][:, :RET_COLS], w_in[l][:, o_conv:], w_in[l][:, o_rwkv:o_conv]],
                                axis=1).astype(BF16)
        b_cat = jnp.concatenate([b_gate[l], jnp.zeros((IN_COLS,), F32)])[None, :]
        pg = _proj(xt, norm1_g[l][None, :], shift1, scale1, w_cat, b_cat, bsz, seq)
        y_ret = _retention(pg, cosf, sinf, ret_tables, ret_gn_g[l][None, :], bsz, seq)
        y_rwkv = _rwkv(pg, rwkv_mu[l], rwkv_w0[l], rwkv_w_lora[l], rwkv_a0[l], rwkv_a_lora[l], rwkv_g_lora[l],
                       rwkv_k_k[l], rwkv_k_a[l], rwkv_r_k[l].reshape(-1), rwkv_gn_g[l], bsz, seq)
        w_route = jnp.zeros((d, LANES), F32)
        w_route = w_route.at[:, 0:N_GROUPS].set(w_router_group[l])
        w_route = w_route.at[:, ROUTE_LANE0:ROUTE_LANE0 + N_EXPERTS].set(w_router_expert[l])
        b_route = jnp.zeros((1, LANES), F32)
        b_route = b_route.at[0, 0:N_GROUPS].set(b_router_group[l])
        b_route = b_route.at[0, ROUTE_LANE0:ROUTE_LANE0 + N_EXPERTS].set(b_router_expert[l])
        x1, h2, route, counts = _merge(pg, y_ret, y_rwkv, xt, conv_w[l], w_branch[l].astype(BF16),
                                       w_o[l].astype(BF16), gate1, norm2_g[l][None, :], shift2, scale2,
                                       w_route, b_route, bsz, seq)
        xt = _moe(x1, h2, route, counts, gate2, final_norm_g[None, :], w_exp_gate[l].astype(BF16),
                  w_exp_up[l].astype(BF16), w_exp_down[l].astype(BF16), seq, final=(l == n_layers - 1))
    return xt.reshape(bsz, seq, d)
```

```python
import functools
import math

import jax
import jax.numpy as jnp
from jax import lax
from jax.experimental import pallas as pl
from jax.experimental.pallas import tpu as pltpu

F32 = jnp.float32
BF16 = jnp.bfloat16
HI = lax.Precision.HIGHEST

D_MODEL = 1024
MIX_WIDTH = D_MODEL // 2
RET_HEAD_DIM = 128
RET_HEADS = MIX_WIDTH // RET_HEAD_DIM
RET_CHUNK = 128
ROPE_BASE = 10000.0
RWKV_HEAD_DIM = 64
RWKV_HEADS = MIX_WIDTH // RWKV_HEAD_DIM
RWKV_PAIRS = RWKV_HEADS // 2
RWKV_CHUNK = 64
DECAY_LORA = 64
AAA_LORA = 64
GATE_LORA = 128
LORA_COLS = DECAY_LORA + AAA_LORA + GATE_LORA
N_BRANCHES = 3
RET_COLS = 4 * MIX_WIDTH
CONV_COLS = 3 * MIX_WIDTH
RWKV_COLS = 3 * MIX_WIDTH + LORA_COLS
IN_COLS = RET_COLS + RWKV_COLS + CONV_COLS
GATE_COLS = N_BRANCHES * D_MODEL
N_GROUPS = 4
EXPERTS_PER_GROUP = 8
N_EXPERTS = N_GROUPS * EXPERTS_PER_GROUP
EXPERT_FF = D_MODEL // 2
NORM_EPS = 1e-6
HEAD_NORM_EPS = 1e-5
RWKV_NORM_EPS = 64e-5

LANES = 128
SUBLANES = 8
VMEM_LIMIT = 56 * 1024 * 1024

COL_GATE = 0
COL_RET = GATE_COLS
COL_CONV = COL_RET + RET_COLS
COL_RWKV = COL_CONV + CONV_COLS
COL_LORA = COL_RWKV + 3 * MIX_WIDTH
PROJ_COLS = IN_COLS + GATE_COLS

PROJ_TM = 2048
PROJ_TN = 768
RET_TB = 512
RWKV_TB = 256
MERGE_TM = 256
ROW_TILE = 256
EXPERT_ROWS = 256
ROUTE_LANE0 = N_GROUPS


def _params(sem, vmem=VMEM_LIMIT):
    return pltpu.CompilerParams(dimension_semantics=sem, vmem_limit_bytes=vmem)


def _nt(a, b):
    return lax.dot_general(a, b, (((1,), (1,)), ((), ())), preferred_element_type=F32)


def _tn(a, b):
    return lax.dot_general(a, b, (((0,), (0,)), ((), ())), preferred_element_type=F32)


def _mm(a, b):
    return jnp.dot(a, b, preferred_element_type=F32)


def _mm_hi(a, b):
    return jnp.dot(a, b, precision=HI, preferred_element_type=F32)


def _mm_split(a, b01):
    hi = a.astype(BF16)
    lo = (a - hi.astype(F32)).astype(BF16)
    return _mm(hi, b01) + _mm(lo, b01)


def _mm_split_left(a01, b):
    hi = b.astype(BF16)
    lo = (b - hi.astype(F32)).astype(BF16)
    return _mm(a01, hi) + _mm(a01, lo)


def _ada_kernel(c_ref, w_ref, b_ref, o_ref):
    c = c_ref[...]
    s = c * jax.nn.sigmoid(c)
    o_ref[0] = _mm_hi(s, w_ref[0]) + b_ref[0]


def _ada(c, w_ada, b_ada):
    n_l, d, d6 = w_ada.shape
    bsz = c.shape[0]
    return pl.pallas_call(
        _ada_kernel,
        grid=(n_l, d6 // d),
        in_specs=[pl.BlockSpec((bsz, d), lambda l, j: (0, 0)),
                  pl.BlockSpec((1, d, d), lambda l, j: (l, 0, j)),
                  pl.BlockSpec((1, 1, d), lambda l, j: (l, 0, j))],
        out_specs=pl.BlockSpec((1, bsz, d), lambda l, j: (l, 0, j)),
        out_shape=jax.ShapeDtypeStruct((n_l, bsz, d6), F32),
        compiler_params=_params(("arbitrary", "arbitrary")),
        name="ada",
    )(c, w_ada, b_ada.reshape(n_l, 1, d6))


def _rope_kernel(pos_ref, invf_ref, sign_ref, cos_ref, sin_ref):
    ang = pos_ref[...] * invf_ref[...]
    cos_ref[...] = jnp.cos(ang)
    sin_ref[...] = jnp.sin(ang) * sign_ref[...]


def _rope_tables(positions):
    n_tok = positions.size
    half = RET_HEAD_DIM // 2
    inv_freq = ROPE_BASE ** (-jnp.arange(half, dtype=F32) / half)
    invf = jnp.concatenate([inv_freq, inv_freq])[None, :]
    sign = jnp.concatenate([-jnp.ones((half,), F32), jnp.ones((half,), F32)])[None, :]
    pos = jnp.broadcast_to(positions.astype(F32).reshape(n_tok, 1), (n_tok, RET_HEAD_DIM))
    tm = 1024
    row = pl.BlockSpec((tm, RET_HEAD_DIM), lambda i: (i, 0))
    const = pl.BlockSpec((1, RET_HEAD_DIM), lambda i: (0, 0))
    return pl.pallas_call(
        _rope_kernel,
        grid=(n_tok // tm,),
        in_specs=[row, const, const],
        out_specs=[row, row],
        out_shape=[jax.ShapeDtypeStruct((n_tok, RET_HEAD_DIM), F32)] * 2,
        compiler_params=_params(("arbitrary",)),
        name="rope",
    )(pos, invf, sign)


def _norm_mod(x, g, shift, scale):
    ms = jnp.mean(x * x, axis=-1, keepdims=True)
    return (x * lax.rsqrt(ms + NORM_EPS) * g) * (1.0 + scale) + shift


def _proj_kernel(x_ref, g_ref, shift_ref, scale_ref, w_ref, b_ref, o_ref, h_ref, *, n_gate):
    j = pl.program_id(2)

    @pl.when(j == 0)
    def _():
        h_ref[...] = _norm_mod(x_ref[...], g_ref[...], shift_ref[0], scale_ref[0]).astype(BF16)

    acc = _mm(h_ref[...], w_ref[...]) + b_ref[...]

    @pl.when(j < n_gate)
    def _():
        o_ref[...] = jax.nn.sigmoid(acc).astype(o_ref.dtype)

    @pl.when(j >= n_gate)
    def _():
        o_ref[...] = acc.astype(o_ref.dtype)


def _proj(x, norm_g, shift, scale, w_cat, b_cat, bsz, seq):
    d = x.shape[-1]
    tm = min(PROJ_TM, seq)
    nt = seq // tm
    return pl.pallas_call(
        functools.partial(_proj_kernel, n_gate=GATE_COLS // PROJ_TN),
        grid=(bsz, nt, PROJ_COLS // PROJ_TN),
        in_specs=[pl.BlockSpec((tm, d), lambda b, i, j: (b * nt + i, 0)),
                  pl.BlockSpec((1, d), lambda b, i, j: (0, 0)),
                  pl.BlockSpec((1, 1, d), lambda b, i, j: (b, 0, 0)),
                  pl.BlockSpec((1, 1, d), lambda b, i, j: (b, 0, 0)),
                  pl.BlockSpec((d, PROJ_TN), lambda b, i, j: (0, j)),
                  pl.BlockSpec((1, PROJ_TN), lambda b, i, j: (0, j))],
        out_specs=pl.BlockSpec((tm, PROJ_TN), lambda b, i, j: (b * nt + i, j)),
        out_shape=jax.ShapeDtypeStruct((bsz * seq, PROJ_COLS), BF16),
        scratch_shapes=[pltpu.VMEM((tm, d), BF16)],
        compiler_params=_params(("arbitrary", "arbitrary", "arbitrary")),
        name="proj",
    )(x, norm_g, shift, scale, w_cat, b_cat)


def _ret_tables():
    log_gamma = jnp.log1p(-jnp.exp2(-5.0 - jnp.arange(RET_HEADS, dtype=F32)))
    pos = jnp.arange(RET_CHUNK, dtype=F32)
    rel = pos[:, None] - pos[None, :]
    decay_in = jnp.where(rel >= 0, jnp.exp(log_gamma[:, None, None] * jnp.maximum(rel, 0.0)), 0.0)
    decay_k = jnp.exp(log_gamma[:, None] * (RET_CHUNK - 1 - pos))
    decay_q = jnp.exp(log_gamma[:, None] * (pos + 1.0))
    decay_chunk = jnp.exp(log_gamma * RET_CHUNK)
    bcast = lambda t: jnp.broadcast_to(t[:, :, None], (RET_HEADS, RET_CHUNK, RET_HEAD_DIM))
    dc = jnp.broadcast_to(decay_chunk[:, None, None], (RET_HEADS, 1, RET_HEAD_DIM))
    return decay_in, bcast(decay_k), bcast(decay_q), dc


def _ret_kernel(q_ref, k_ref, v_ref, g_ref, cos_ref, sin_ref, din_ref, zeta_ref, xi_ref, dc_ref, gn_ref,
                o_ref, state_ref, *, tb):
    @pl.when(pl.program_id(1) == 0)
    def _():
        state_ref[...] = jnp.zeros_like(state_ref)

    k_scale = RET_HEAD_DIM ** -0.5
    for c in range(tb // RET_CHUNK):
        rows = slice(c * RET_CHUNK, (c + 1) * RET_CHUNK)
        cosf = cos_ref[rows, :]
        sinf = sin_ref[rows, :]
        for h in range(RET_HEADS):
            cols = slice(h * RET_HEAD_DIM, (h + 1) * RET_HEAD_DIM)
            q = q_ref[rows, cols].astype(F32)
            k = k_ref[rows, cols].astype(F32)
            qr = q * cosf + pltpu.roll(q, RET_HEAD_DIM // 2, 1) * sinf
            kr = (k * cosf + pltpu.roll(k, RET_HEAD_DIM // 2, 1) * sinf) * k_scale
            qb = qr.astype(BF16)
            vb = v_ref[rows, cols].astype(BF16)
            scores = _nt(qb, kr.astype(BF16)) * din_ref[h]
            inner = _mm(scores.astype(BF16), vb)
            st = state_ref[h]
            cross = _mm(qb, st.astype(BF16)) * xi_ref[h]
            state_ref[h] = st * dc_ref[h] + _tn((kr * zeta_ref[h]).astype(BF16), vb)
            out = inner + cross
            mu = jnp.mean(out, axis=-1, keepdims=True)
            dev = out - mu
            var = jnp.mean(dev * dev, axis=-1, keepdims=True)
            g = g_ref[rows, cols].astype(F32)
            y = (g * jax.nn.sigmoid(g)) * (dev * lax.rsqrt(var + HEAD_NORM_EPS) * gn_ref[:, cols])
            o_ref[rows, cols] = y.astype(o_ref.dtype)


def _retention(pg, cosf, sinf, tables, gn_g, bsz, seq):
    tb = min(RET_TB, seq)
    nt = seq // tb
    mw = MIX_WIDTH
    col = lambda cb: pl.BlockSpec((tb, mw), lambda b, i, cb=cb: (b * nt + i, cb))
    rope = pl.BlockSpec((tb, RET_HEAD_DIM), lambda b, i: (b * nt + i, 0))
    tab = pl.BlockSpec((RET_HEADS, RET_CHUNK, RET_HEAD_DIM), lambda b, i: (0, 0, 0))
    c0 = COL_RET // mw
    return pl.pallas_call(
        functools.partial(_ret_kernel, tb=tb),
        grid=(bsz, nt),
        in_specs=[col(c0), col(c0 + 1), col(c0 + 2), col(c0 + 3), rope, rope, tab, tab, tab,
                  pl.BlockSpec((RET_HEADS, 1, RET_HEAD_DIM), lambda b, i: (0, 0, 0)),
                  pl.BlockSpec((1, mw), lambda b, i: (0, 0))],
        out_specs=pl.BlockSpec((tb, mw), lambda b, i: (b * nt + i, 0)),
        out_shape=jax.ShapeDtypeStruct((bsz * seq, mw), BF16),
        scratch_shapes=[pltpu.VMEM((RET_HEADS, RET_HEAD_DIM, RET_HEAD_DIM), F32)],
        compiler_params=_params(("arbitrary", "arbitrary")),
        name="ret",
    )(pg, pg, pg, pg, cosf, sinf, *tables, gn_g)


def _rwkv_kernel(r_ref, k_ref, v_ref, lo_ref, mur_ref, muk_ref, muv_ref, mulo_ref, w0_ref, a0_ref,
                 wl_ref, al_ref, gl_ref, kk_ref, ka_ref, rk_ref, gn_ref, tri_ref, ones_ref,
                 o_ref, cr, ck, cv, clo, state, s_rt, s_at, s_kt, s_bt, s_kw, s_bw, s_v, s_g, s_bonus, s_y,
                 *, tb):
    @pl.when(pl.program_id(1) == 0)
    def _():
        for ref in (cr, ck, cv, clo, state):
            ref[...] = jnp.zeros_like(ref)

    def shift(ref, carry, mu_ref):
        p = ref[...].astype(F32)
        row = lax.broadcasted_iota(jnp.int32, p.shape, 0)
        prev = jnp.where(row == 0, carry[SUBLANES - 1:SUBLANES, :], pltpu.roll(p, 1, 0))
        carry[...] = p[tb - SUBLANES:tb, :]
        return p + mu_ref[...] * (prev - p)

    r = shift(r_ref, cr, mur_ref)
    k = shift(k_ref, ck, muk_ref)
    v = shift(v_ref, cv, muv_ref)
    lo = shift(lo_ref, clo, mulo_ref)
    wd = lo[:, 0:DECAY_LORA]
    ad = lo[:, DECAY_LORA:DECAY_LORA + AAA_LORA]
    gd = lo[:, DECAY_LORA + AAA_LORA:LORA_COLS]
    logw = -math.exp(-0.5) * jax.nn.sigmoid(w0_ref[...] + _mm(jnp.tanh(wd).astype(BF16), wl_ref[...]))
    a = jax.nn.sigmoid(a0_ref[...] + _mm(ad.astype(BF16), al_ref[...]))
    s_g[...] = _mm(jax.nn.sigmoid(gd).astype(BF16), gl_ref[...])
    kk = k * kk_ref[...]
    kh = kk * lax.rsqrt(jnp.maximum(_mm((kk * kk).astype(BF16), ones_ref[...]), 1e-12))
    kt = k * (1.0 + (a - 1.0) * ka_ref[...])
    s_bonus[...] = _mm((r * kt * rk_ref[...]).astype(BF16), ones_ref[...]) * v
    ch = RWKV_CHUNK
    hd = RWKV_HEAD_DIM
    n_ch = tb // ch
    mw = MIX_WIDTH
    b = a * kh
    cum = _mm_split_left(tri_ref[...], logw)
    ends = [cum[(c + 1) * ch - 1:(c + 1) * ch, :] for c in range(n_ch)]
    cend = jnp.concatenate([jnp.broadcast_to(e, (ch, mw)) for e in ends], axis=0)
    e_inv = jnp.exp(-cum)
    e_end = jnp.exp(cend - cum)
    s_rt[...] = r * jnp.exp(cum)
    s_at[...] = kh * jnp.exp(cum - logw)
    s_kt[...] = kt * e_inv
    s_bt[...] = b * e_inv
    s_kw[...] = kt * e_end
    s_bw[...] = b * e_end
    s_v[...] = v

    lane = lax.broadcasted_iota(jnp.int32, (ch, LANES), 1)
    rowi = lax.broadcasted_iota(jnp.int32, (ch, LANES), 0)
    strict = (rowi > (lane % hd)).astype(F32)
    incl = (rowi >= (lane % hd)).astype(F32)
    lane2 = lax.broadcasted_iota(jnp.int32, (2 * ch, LANES), 1)
    row2 = lax.broadcasted_iota(jnp.int32, (2 * ch, LANES), 0)
    first2 = lane2 < hd
    blockdiag = ((row2 < hd) == (lane2 < hd)).astype(F32)
    zeros_sq = jnp.zeros((2 * ch, LANES), BF16)

    def bd(x, swap):
        lane_x = lax.broadcasted_iota(jnp.int32, x.shape, 1)
        first = (lane_x % LANES) < hd
        top = jnp.where(first, 0.0, x) if swap else jnp.where(first, x, 0.0)
        bot = jnp.where(first, x, 0.0) if swap else jnp.where(first, 0.0, x)
        return jnp.concatenate([top, bot], axis=0).astype(BF16)

    items = [(c, p) for c in range(n_ch) for p in range(RWKV_PAIRS)]
    rows = lambda c: slice(c * ch, (c + 1) * ch)
    cols = lambda p: slice(p * LANES, (p + 1) * LANES)
    blk = lambda ref, i: ref[rows(i[0]), cols(i[1])]
    a_t = [blk(s_at, i) for i in items]
    r_t = [blk(s_rt, i) for i in items]
    k_t = [blk(s_kt, i) for i in items]
    b_t = [blk(s_bt, i) for i in items]
    n_it = range(len(items))
    ar = [jnp.concatenate([a_t[i], r_t[i]], axis=0) for i in n_it]
    kb = [jnp.concatenate([k_t[i], b_t[i]], axis=0).astype(BF16) for i in n_it]
    bk = [jnp.concatenate([b_t[i], k_t[i]], axis=0).astype(BF16) for i in n_it]
    m_a = [_nt(jnp.where(first2, ar[i], 0.0).astype(BF16), kb[i]) for i in n_it]
    m_b = [_nt(jnp.where(first2, 0.0, ar[i]).astype(BF16), bk[i]) for i in n_it]
    k_sc = [jnp.where(first2, m_a[i], m_b[i]) for i in n_it]
    b_sc = [jnp.where(first2, m_b[i], m_a[i]) for i in n_it]
    a_ak = [k_sc[i][0:ch] * strict for i in n_it]
    n_ab = [b_sc[i][0:ch] * strict for i in n_it]
    a_rk = [k_sc[i][ch:2 * ch] * incl for i in n_it]
    a_rb = [b_sc[i][ch:2 * ch] * incl for i in n_it]
    v2 = [blk(s_v, i) for i in items]
    v_bd = [bd(v2[i], False) for i in n_it]
    x = [jnp.concatenate([_mm(a_ak[i].astype(BF16), v_bd[i]), a_t[i]], axis=1) for i in n_it]
    eye = (rowi == (lane % hd)).astype(F32)
    m = [-n_ab[i] for i in n_it]
    t = [eye + m[i] for i in n_it]
    m = [_mm(m[i].astype(BF16), bd(m[i], False)) for i in n_it]
    for j in range(1, 6):
        if j < 5:
            res = [_mm(m[i].astype(BF16), jnp.concatenate([bd(t[i], False), bd(m[i], False)], axis=1))
                   for i in n_it]
            t = [t[i] + res[i][:, 0:LANES] for i in n_it]
            m = [res[i][:, LANES:2 * LANES] for i in n_it]
        else:
            t = [t[i] + _mm(m[i].astype(BF16), bd(t[i], False)) for i in n_it]
    x = [_mm(t[i].astype(BF16), bd(x[i], True)) for i in n_it]
    u_loc = [x[i][:, 0:LANES] for i in n_it]
    z = [x[i][:, LANES:2 * LANES] for i in n_it]
    yg = [_mm(jnp.concatenate([a_rk[i], -a_rb[i]], axis=1).astype(BF16),
              jnp.concatenate([jnp.concatenate([v_bd[i], zeros_sq], axis=1), bd(x[i], True)], axis=0))
          for i in n_it]
    y_loc = [yg[i][:, 0:LANES] for i in n_it]
    g = [(r_t[i] + yg[i][:, LANES:2 * LANES]).astype(BF16) for i in n_it]
    q = [_tn(jnp.concatenate([v2[i], u_loc[i]], axis=0).astype(BF16),
             jnp.concatenate([blk(s_kw, items[i]), -blk(s_bw, items[i])], axis=0).astype(BF16)) * blockdiag
         for i in n_it]
    pm = [(_tn(z[i].astype(BF16), blk(s_bw, items[i]).astype(BF16)) * blockdiag).astype(BF16) for i in n_it]

    for p in range(RWKV_PAIRS):
        st = state[p]
        for c in range(n_ch):
            i = c * RWKV_PAIRS + p
            sb = st.astype(BF16)
            s_y[rows(c), cols(p)] = y_loc[i] + _nt(g[i], sb)
            st = st * jnp.exp(ends[c][:, cols(p)]) - _mm(sb, pm[i]) + q[i]
        state[p] = st

    y = s_y[...]
    inv_n = 1.0 / hd
    mu = _mm_split(y, ones_ref[...]) * inv_n
    dev = y - mu
    var = _mm((dev * dev).astype(BF16), ones_ref[...]) * inv_n
    yn = dev * lax.rsqrt(var + RWKV_NORM_EPS) * gn_ref[...]
    o_ref[...] = ((yn + s_bonus[...]) * s_g[...]).astype(o_ref.dtype)


def _rwkv(pg, mu, w0, w_lora, a0, a_lora, g_lora, k_k, k_a, r_k, gn_g, bsz, seq):
    tb = min(RWKV_TB, seq)
    nt = seq // tb
    mw = MIX_WIDTH
    col = lambda cb: pl.BlockSpec((tb, mw), lambda b, i, cb=cb: (b * nt + i, cb))
    c0 = COL_RWKV // mw
    const = lambda shape: pl.BlockSpec(shape, lambda b, i: (0,) * len(shape))
    row = lambda t: t.reshape(1, -1)
    idx = jnp.arange(tb)
    same_chunk = (idx[:, None] // RWKV_CHUNK) == (idx[None, :] // RWKV_CHUNK)
    tri = ((idx[:, None] >= idx[None, :]) & same_chunk).astype(BF16)
    hid = jnp.arange(mw) // RWKV_HEAD_DIM
    ones = (hid[:, None] == hid[None, :]).astype(BF16)
    vec = const((1, mw))
    buf = pltpu.VMEM((tb, mw), F32)
    return pl.pallas_call(
        functools.partial(_rwkv_kernel, tb=tb),
        grid=(bsz, nt),
        in_specs=[col(c0), col(c0 + 1), col(c0 + 2),
                  pl.BlockSpec((tb, LORA_COLS), lambda b, i: (b * nt + i, COL_LORA // LORA_COLS)),
                  vec, vec, vec, const((1, LORA_COLS)), vec, vec,
                  const((DECAY_LORA, mw)), const((AAA_LORA, mw)), const((GATE_LORA, mw)),
                  vec, vec, vec, vec, const((tb, tb)), const((mw, mw))],
        out_specs=pl.BlockSpec((tb, mw), lambda b, i: (b * nt + i, 0)),
        out_shape=jax.ShapeDtypeStruct((bsz * seq, mw), BF16),
        scratch_shapes=[pltpu.VMEM((SUBLANES, mw), F32)] * 3 + [pltpu.VMEM((SUBLANES, LORA_COLS), F32),
                        pltpu.VMEM((RWKV_PAIRS, LANES, LANES), F32)] + [buf] * 10,
        compiler_params=_params(("arbitrary", "arbitrary")),
        name="rwkv",
    )(pg, pg, pg, pg, row(mu[0:mw]), row(mu[mw:2 * mw]), row(mu[2 * mw:3 * mw]), row(mu[3 * mw:]),
      row(w0), row(a0), w_lora.astype(BF16), a_lora.astype(BF16), g_lora.astype(BF16),
      row(k_k), row(k_a), row(r_k), row(gn_g), tri, ones)


def _merge_kernel(ch_ref, cb_ref, cc_ref, g0_ref, g1_ref, g2_ref, yret_ref, yrwkv_ref, x_ref, convw_ref,
                  wb_ref, wo_ref, gate1_ref, n2g_ref, shift2_ref, scale2_ref, wr_ref, br_ref, tri_ref,
                  x1_ref, h2_ref, route_ref, cnt_ref, ucarry, cnt_scr, *, tm):
    first_step = (pl.program_id(0) == 0) & (pl.program_id(1) == 0)

    @pl.when(first_step)
    def _():
        cnt_scr[...] = jnp.zeros_like(cnt_scr)

    @pl.when(pl.program_id(1) == 0)
    def _():
        ucarry[...] = jnp.zeros_like(ucarry)

    u = cc_ref[...].astype(F32) * ch_ref[...].astype(F32)
    row = lax.broadcasted_iota(jnp.int32, u.shape, 0)
    last = ucarry[SUBLANES - 1:SUBLANES, :]
    u1 = jnp.where(row == 0, last, pltpu.roll(u, 1, 0))
    u2 = jnp.where(row == 0, ucarry[SUBLANES - 2:SUBLANES - 1, :], jnp.where(row == 1, last, pltpu.roll(u, 2, 0)))
    ucarry[...] = u[tm - SUBLANES:tm, :]
    y_conv = cb_ref[...].astype(F32) * (convw_ref[0:1, :] * u2 + convw_ref[1:2, :] * u1 + convw_ref[2:3, :] * u)

    merged = g0_ref[...].astype(F32) * _mm(yret_ref[...], wb_ref[0])
    merged = merged + g1_ref[...].astype(F32) * _mm(yrwkv_ref[...], wb_ref[1])
    merged = merged + g2_ref[...].astype(F32) * _mm(y_conv.astype(BF16), wb_ref[2])
    x1 = x_ref[...] + gate1_ref[0] * _mm(merged.astype(BF16), wo_ref[...])
    x1_ref[...] = x1
    h2 = _norm_mod(x1, n2g_ref[...], shift2_ref[0], scale2_ref[0])
    bits = lax.bitcast_convert_type(h2.astype(BF16).astype(F32), jnp.uint32)
    half = bits.shape[1] // 2
    h2_ref[...] = (bits[:, :half] >> 16) | (bits[:, half:] & jnp.uint32(0xFFFF0000))

    h_hi = h2.astype(BF16)
    h_lo = (h2 - h_hi.astype(F32)).astype(BF16)
    logits = _mm(h_hi, wr_ref[0]) + _mm(h_lo, wr_ref[0]) + _mm(h_hi, wr_ref[1]) + br_ref[...]
    lane = lax.broadcasted_iota(jnp.int32, logits.shape, 1)
    lanef = lane.astype(F32)
    neg = -jnp.inf
    big = 1e9
    is_group = lane < N_GROUPS
    gl = jnp.where(is_group, logits, neg)
    gmax = jnp.max(gl, axis=-1, keepdims=True)
    gsel = jnp.min(jnp.where(gl == gmax, lanef, big), axis=-1, keepdims=True)
    p_group = 1.0 / jnp.sum(jnp.where(is_group, jnp.exp(gl - gmax), 0.0), axis=-1, keepdims=True)
    lo = ROUTE_LANE0 + EXPERTS_PER_GROUP * gsel
    el = jnp.where((lanef >= lo) & (lanef < lo + EXPERTS_PER_GROUP), logits, neg)
    m1 = jnp.max(el, axis=-1, keepdims=True)
    i1 = jnp.min(jnp.where(el == m1, lanef, big), axis=-1, keepdims=True)
    el2 = jnp.where(lanef == i1, neg, el)
    m2 = jnp.max(el2, axis=-1, keepdims=True)
    i2 = jnp.min(jnp.where(el2 == m2, lanef, big), axis=-1, keepdims=True)
    t = jnp.exp(m2 - m1)
    w1 = p_group / (1.0 + t)
    w2 = p_group * t / (1.0 + t)

    hit1 = lanef == i1
    hit2 = lanef == i2
    onehot = jnp.where(hit1 | hit2, 1.0, 0.0)
    before = _mm(tri_ref[...], onehot.astype(BF16)) + cnt_scr[0:1, :]
    rank1 = jnp.sum(jnp.where(hit1, before, 0.0), axis=-1, keepdims=True)
    rank2 = jnp.sum(jnp.where(hit2, before, 0.0), axis=-1, keepdims=True)
    cnt_scr[...] = cnt_scr[...] + jnp.sum(onehot, axis=0, keepdims=True)
    cnt_ref[...] = cnt_scr[...]

    out = jnp.where(lane == 0, i1 - ROUTE_LANE0, 0.0)
    out = jnp.where(lane == 1, i2 - ROUTE_LANE0, out)
    out = jnp.where(lane == 2, w1, out)
    out = jnp.where(lane == 3, w2, out)
    out = jnp.where(lane == 4, rank1, out)
    out = jnp.where(lane == 5, rank2, out)
    route_ref[...] = out


def _merge(pg, y_ret, y_rwkv, x, conv_w, w_branch, w_o, gate1, norm2_g, shift2, scale2, w_route, b_route,
           bsz, seq):
    tm = min(MERGE_TM, seq)
    nt = seq // tm
    d = D_MODEL
    mw = MIX_WIDTH
    n_tok = bsz * seq
    tok = lambda width, cb=0: pl.BlockSpec((tm, width), lambda b, i, cb=cb: (b * nt + i, cb))
    const = lambda shape: pl.BlockSpec(shape, lambda b, i: (0,) * len(shape))
    per_b = pl.BlockSpec((1, 1, d), lambda b, i: (b, 0, 0))
    cc0 = COL_CONV // mw
    gc0 = COL_GATE // d
    idx = jnp.arange(tm)
    tri = (idx[:, None] > idx[None, :]).astype(BF16)
    convw = jnp.zeros((SUBLANES, mw), F32).at[0:conv_w.shape[0]].set(conv_w)
    return pl.pallas_call(
        functools.partial(_merge_kernel, tm=tm),
        grid=(bsz, nt),
        in_specs=[tok(mw, cc0), tok(mw, cc0 + 1), tok(mw, cc0 + 2),
                  tok(d, gc0), tok(d, gc0 + 1), tok(d, gc0 + 2),
                  tok(mw), tok(mw), tok(d), const((SUBLANES, mw)),
                  const((N_BRANCHES, mw, d)), const((d, d)), per_b, const((1, d)), per_b, per_b,
                  const((2, d, LANES)), const((1, LANES)), const((tm, tm))],
        out_specs=[tok(d), tok(d // 2), tok(LANES), const((SUBLANES, LANES))],
        out_shape=[jax.ShapeDtypeStruct((n_tok, d), F32), jax.ShapeDtypeStruct((n_tok, d // 2), jnp.uint32),
                   jax.ShapeDtypeStruct((n_tok, LANES), F32), jax.ShapeDtypeStruct((SUBLANES, LANES), F32)],
        scratch_shapes=[pltpu.VMEM((SUBLANES, mw), F32), pltpu.VMEM((SUBLANES, LANES), F32)],
        compiler_params=_params(("arbitrary", "arbitrary")),
        name="merge",
    )(pg, pg, pg, pg, pg, pg, y_ret, y_rwkv, x, convw, w_branch, w_o, gate1, norm2_g, shift2, scale2,
      jnp.stack([w_route.astype(BF16), (w_route - w_route.astype(BF16).astype(F32)).astype(BF16)]), b_route, tri)


def _row_copy(src, src_row, dst, dst_row, sem):
    return pltpu.make_async_copy(src.at[pl.ds(src_row, 1), :], dst.at[pl.ds(dst_row, 1), :], sem)


def _dispatch_kernel(dest_ref, h2_ref, rows_in_ref, rows_ref, sem, *, tile):
    del rows_in_ref

    def start(r, carry):
        for k in range(2):
            _row_copy(h2_ref, r, rows_ref, dest_ref[0, 0, 2 * r + k], sem).start(priority=k)
        return carry

    def wait(r, carry):
        for k in range(2):
            _row_copy(h2_ref, 0, rows_ref, 0, sem).wait()
        return carry

    lax.fori_loop(0, tile, start, 0, unroll=8)
    lax.fori_loop(0, tile, wait, 0, unroll=8)


def _dispatch(dest, h2, n_rows):
    n_tok, d = h2.shape
    tile = min(ROW_TILE, n_tok)
    nt = n_tok // tile
    zeros = jnp.zeros((n_rows, d), h2.dtype)
    return pl.pallas_call(
        functools.partial(_dispatch_kernel, tile=tile),
        grid=(nt,),
        in_specs=[pl.BlockSpec((1, 1, 2 * tile), lambda i: (i, 0, 0), memory_space=pltpu.SMEM),
                  pl.BlockSpec((tile, d), lambda i: (i, 0)),
                  pl.BlockSpec(memory_space=pl.ANY)],
        out_specs=pl.BlockSpec(memory_space=pl.ANY),
        out_shape=jax.ShapeDtypeStruct((n_rows, d), h2.dtype),
        scratch_shapes=[pltpu.SemaphoreType.DMA(())],
        input_output_aliases={2: 0},
        compiler_params=_params(("arbitrary",)),
        name="dispatch",
    )(dest.reshape(nt, 1, 2 * tile), h2, zeros)


def _expert_kernel(be_ref, nused_ref, x_ref, wg_ref, wu_ref, wd_ref, y_ref):
    used = pl.program_id(0) < nused_ref[0]

    @pl.when(used)
    def _():
        w = x_ref[...]
        xb = jnp.concatenate([lax.bitcast_convert_type(w << 16, F32),
                              lax.bitcast_convert_type(w & jnp.uint32(0xFFFF0000), F32)], axis=1).astype(BF16)
        g = _mm(xb, wg_ref[0])
        u = _mm(xb, wu_ref[0])
        y_ref[...] = _mm(((g * jax.nn.sigmoid(g)) * u).astype(BF16), wd_ref[0])

    @pl.when(jnp.logical_not(used))
    def _():
        y_ref[...] = jnp.zeros_like(y_ref)


def _experts(block_e, n_used, x_rows, w_gate, w_up, w_down):
    n_rows = x_rows.shape[0]
    d, ff = w_gate.shape[-2:]
    grid_spec = pltpu.PrefetchScalarGridSpec(
        num_scalar_prefetch=2,
        grid=(n_rows // EXPERT_ROWS,),
        in_specs=[pl.BlockSpec((EXPERT_ROWS, d // 2), lambda b, be, nu: (b, 0)),
                  pl.BlockSpec((1, d, ff), lambda b, be, nu: (be[b], 0, 0)),
                  pl.BlockSpec((1, d, ff), lambda b, be, nu: (be[b], 0, 0)),
                  pl.BlockSpec((1, ff, d), lambda b, be, nu: (be[b], 0, 0))],
        out_specs=pl.BlockSpec((EXPERT_ROWS, d), lambda b, be, nu: (b, 0)),
    )
    return pl.pallas_call(
        _expert_kernel,
        grid_spec=grid_spec,
        out_shape=jax.ShapeDtypeStruct((n_rows, d), F32),
        compiler_params=_params(("arbitrary",)),
        name="expert",
    )(block_e, n_used, x_rows, w_gate, w_up, w_down)


def _combine_kernel(dest_ref, dest_next_ref, x1_ref, route_ref, gate2_ref, fg_ref, rows_ref, o_ref, ybuf, sem,
                    *, tile, final, n_tiles):
    i = pl.program_id(0)
    slot = i % 2

    def issue(dref, s):
        def start(r, carry):
            for k in range(2):
                _row_copy(rows_ref, dref[0, 0, 2 * r + k], ybuf.at[s, k], r, sem.at[s]).start(priority=k)
            return carry
        lax.fori_loop(0, tile, start, 0, unroll=8)

    @pl.when(i == 0)
    def _():
        issue(dest_ref, 0)

    @pl.when(i + 1 < n_tiles)
    def _():
        issue(dest_next_ref, 1 - slot)

    def wait(r, carry):
        for k in range(2):
            _row_copy(rows_ref, 0, ybuf.at[slot, k], 0, sem.at[slot]).wait()
        return carry

    lax.fori_loop(0, tile, wait, 0, unroll=8)
    route = route_ref[...]
    y = route[:, 2:3] * ybuf[slot, 0] + route[:, 3:4] * ybuf[slot, 1]
    x2 = x1_ref[...] + gate2_ref[0] * y
    if final:
        ms = jnp.mean(x2 * x2, axis=-1, keepdims=True)
        x2 = x2 * lax.rsqrt(ms + NORM_EPS) * fg_ref[...]
    o_ref[...] = x2


def _combine(dest, x1, route, gate2, final_g, y_rows, seq, final):
    n_tok, d = x1.shape
    tile = min(ROW_TILE, seq)
    nt = n_tok // tile
    per_seq = seq // tile
    return pl.pallas_call(
        functools.partial(_combine_kernel, tile=tile, final=final, n_tiles=nt),
        grid=(nt,),
        in_specs=[pl.BlockSpec((1, 1, 2 * tile), lambda i: (i, 0, 0), memory_space=pltpu.SMEM),
                  pl.BlockSpec((1, 1, 2 * tile), lambda i: (jnp.minimum(i + 1, nt - 1), 0, 0),
                               memory_space=pltpu.SMEM),
                  pl.BlockSpec((tile, d), lambda i: (i, 0)),
                  pl.BlockSpec((tile, LANES), lambda i: (i, 0)),
                  pl.BlockSpec((1, 1, d), lambda i: (i // per_seq, 0, 0)),
                  pl.BlockSpec((1, d), lambda i: (0, 0)),
                  pl.BlockSpec(memory_space=pl.ANY)],
        out_specs=pl.BlockSpec((tile, d), lambda i: (i, 0)),
        out_shape=jax.ShapeDtypeStruct((n_tok, d), F32),
        scratch_shapes=[pltpu.VMEM((2, 2, tile, d), F32), pltpu.SemaphoreType.DMA((2,))],
        compiler_params=_params(("arbitrary",)),
        name="combine",
    )(dest.reshape(nt, 1, 2 * tile), dest.reshape(nt, 1, 2 * tile), x1, route, gate2, final_g, y_rows)


def _moe(x1, h2, route, counts, gate2, final_g, w_gate, w_up, w_down, seq, final):
    n_tok = x1.shape[0]
    n_rows = 2 * n_tok + N_EXPERTS * EXPERT_ROWS
    n_blocks = n_rows // EXPERT_ROWS
    cnt = counts[0, ROUTE_LANE0:ROUTE_LANE0 + N_EXPERTS].astype(jnp.int32)
    padded = (cnt + EXPERT_ROWS - 1) // EXPERT_ROWS * EXPERT_ROWS
    pends = jnp.cumsum(padded)
    pstarts = pends - padded
    eid = route[:, 0:2].astype(jnp.int32)
    dest = (pstarts[eid] + route[:, 4:6].astype(jnp.int32)).reshape(-1)
    n_used = (pends[-1] // EXPERT_ROWS).astype(jnp.int32)
    blk = jnp.minimum(jnp.arange(n_blocks, dtype=jnp.int32), n_used - 1)
    block_e = jnp.sum((pends[None, :] <= (blk * EXPERT_ROWS)[:, None]).astype(jnp.int32), axis=1)
    block_e = jnp.minimum(block_e, N_EXPERTS - 1)
    x_rows = _dispatch(dest, h2, n_rows)
    y_rows = _experts(block_e, n_used.reshape(1), x_rows, w_gate, w_up, w_down)
    return _combine(dest, x1, route, gate2, final_g, y_rows, seq, final)


def kernel(x, c, positions, norm1_g, norm2_g, final_norm_g, w_ada, b_ada, w_in, w_gate, b_gate, ret_gn_g, rwkv_mu, rwkv_w0, rwkv_w_lora, rwkv_a0, rwkv_a_lora, rwkv_g_lora, rwkv_k_k, rwkv_k_a, rwkv_r_k, rwkv_gn_g, conv_w, w_branch, w_o, w_router_group, b_router_group, w_router_expert, b_router_expert, w_exp_gate, w_exp_up, w_exp_down):
    bsz, seq, d = x.shape
    assert d == D_MODEL and seq % RET_TB == 0
    n_layers = w_in.shape[0]
    n_tok = bsz * seq
    xt = x.reshape(n_tok, d)
    mod = _ada(c, w_ada, b_ada)
    cosf, sinf = _rope_tables(positions)
    ret_tables = _ret_tables()
    for l in range(n_layers):
        shift1, scale1, gate1, shift2, scale2, gate2 = [
            mod[l, :, i * d:(i + 1) * d].reshape(bsz, 1, d) for i in range(6)]
        o_rwkv = RET_COLS
        o_conv = RET_COLS + RWKV_COLS
        w_cat = jnp.concatenate([w_gate[l], w_in[l][:, :RET_COLS], w_in[l][:, o_conv:], w_in[l][:, o_rwkv:o_conv]],
                                axis=1).astype(BF16)
        b_cat = jnp.concatenate([b_gate[l], jnp.zeros((IN_COLS,), F32)])[None, :]
        pg = _proj(xt, norm1_g[l][None, :], shift1, scale1, w_cat, b_cat, bsz, seq)
        y_ret = _retention(pg, cosf, sinf, ret_tables, ret_gn_g[l][None, :], bsz, seq)
        y_rwkv = _rwkv(pg, rwkv_mu[l], rwkv_w0[l], rwkv_w_lora[l], rwkv_a0[l], rwkv_a_lora[l], rwkv_g_lora[l],
                       rwkv_k_k[l], rwkv_k_a[l], rwkv_r_k[l].reshape(-1), rwkv_gn_g[l], bsz, seq)
        w_route = jnp.zeros((d, LANES), F32)
        w_route = w_route.at[:, 0:N_GROUPS].set(w_router_group[l])
        w_route = w_route.at[:, ROUTE_LANE0:ROUTE_LANE0 + N_EXPERTS].set(w_router_expert[l])
        b_route = jnp.zeros((1, LANES), F32)
        b_route = b_route.at[0, 0:N_GROUPS].set(b_router_group[l])
        b_route = b_route.at[0, ROUTE_LANE0:ROUTE_LANE0 + N_EXPERTS].set(b_router_expert[l])
        x1, h2, route, counts = _merge(pg, y_ret, y_rwkv, xt, conv_w[l], w_branch[l].astype(BF16),
                                       w_o[l].astype(BF16), gate1, norm2_g[l][None, :], shift2, scale2,
                                       w_route, b_route, bsz, seq)
        xt = _moe(x1, h2, route, counts, gate2, final_norm_g[None, :], w_exp_gate[l].astype(BF16),
                  w_exp_up[l].astype(BF16), w_exp_down[l].astype(BF16), seq, final=(l == n_layers - 1))
    return xt.reshape(bsz, seq, d)
```

```python
import functools
import math

import jax
import jax.numpy as jnp
from jax import lax
from jax.experimental import pallas as pl
from jax.experimental.pallas import tpu as pltpu

F32 = jnp.float32
BF16 = jnp.bfloat16
HI = lax.Precision.HIGHEST

D_MODEL = 1024
MIX_WIDTH = D_MODEL // 2
RET_HEAD_DIM = 128
RET_HEADS = MIX_WIDTH // RET_HEAD_DIM
RET_CHUNK = 128
ROPE_BASE = 10000.0
RWKV_HEAD_DIM = 64
RWKV_HEADS = MIX_WIDTH // RWKV_HEAD_DIM
RWKV_PAIRS = RWKV_HEADS // 2
RWKV_CHUNK = 64
DECAY_LORA = 64
AAA_LORA = 64
GATE_LORA = 128
LORA_COLS = DECAY_LORA + AAA_LORA + GATE_LORA
N_BRANCHES = 3
RET_COLS = 4 * MIX_WIDTH
CONV_COLS = 3 * MIX_WIDTH
RWKV_COLS = 3 * MIX_WIDTH + LORA_COLS
IN_COLS = RET_COLS + RWKV_COLS + CONV_COLS
GATE_COLS = N_BRANCHES * D_MODEL
N_GROUPS = 4
EXPERTS_PER_GROUP = 8
N_EXPERTS = N_GROUPS * EXPERTS_PER_GROUP
EXPERT_FF = D_MODEL // 2
NORM_EPS = 1e-6
HEAD_NORM_EPS = 1e-5
RWKV_NORM_EPS = 64e-5

LANES = 128
SUBLANES = 8
VMEM_LIMIT = 56 * 1024 * 1024

COL_GATE = 0
COL_RET = GATE_COLS
COL_CONV = COL_RET + RET_COLS
COL_RWKV = COL_CONV + CONV_COLS
COL_LORA = COL_RWKV + 3 * MIX_WIDTH
PROJ_COLS = IN_COLS + GATE_COLS

PROJ_TM = 2048
PROJ_TN = 768
RET_TB = 512
RWKV_TB = 256
MERGE_TM = 256
ROW_TILE = 256
EXPERT_ROWS = 512
ROUTE_LANE0 = N_GROUPS


def _params(sem, vmem=VMEM_LIMIT):
    return pltpu.CompilerParams(dimension_semantics=sem, vmem_limit_bytes=vmem)


def _nt(a, b):
    return lax.dot_general(a, b, (((1,), (1,)), ((), ())), preferred_element_type=F32)


def _tn(a, b):
    return lax.dot_general(a, b, (((0,), (0,)), ((), ())), preferred_element_type=F32)


def _mm(a, b):
    return jnp.dot(a, b, preferred_element_type=F32)


def _mm_hi(a, b):
    return jnp.dot(a, b, precision=HI, preferred_element_type=F32)


def _mm_split(a, b01):
    hi = a.astype(BF16)
    lo = (a - hi.astype(F32)).astype(BF16)
    return _mm(hi, b01) + _mm(lo, b01)


def _mm_split_left(a01, b):
    hi = b.astype(BF16)
    lo = (b - hi.astype(F32)).astype(BF16)
    return _mm(a01, hi) + _mm(a01, lo)


def _ada_kernel(c_ref, w_ref, b_ref, o_ref):
    c = c_ref[...]
    s = c * jax.nn.sigmoid(c)
    o_ref[0] = _mm_hi(s, w_ref[0]) + b_ref[0]


def _ada(c, w_ada, b_ada):
    n_l, d, d6 = w_ada.shape
    bsz = c.shape[0]
    return pl.pallas_call(
        _ada_kernel,
        grid=(n_l, d6 // d),
        in_specs=[pl.BlockSpec((bsz, d), lambda l, j: (0, 0)),
                  pl.BlockSpec((1, d, d), lambda l, j: (l, 0, j)),
                  pl.BlockSpec((1, 1, d), lambda l, j: (l, 0, j))],
        out_specs=pl.BlockSpec((1, bsz, d), lambda l, j: (l, 0, j)),
        out_shape=jax.ShapeDtypeStruct((n_l, bsz, d6), F32),
        compiler_params=_params(("arbitrary", "arbitrary")),
        name="ada",
    )(c, w_ada, b_ada.reshape(n_l, 1, d6))


def _rope_kernel(pos_ref, invf_ref, sign_ref, cos_ref, sin_ref):
    ang = pos_ref[...] * invf_ref[...]
    cos_ref[...] = jnp.cos(ang)
    sin_ref[...] = jnp.sin(ang) * sign_ref[...]


def _rope_tables(positions):
    n_tok = positions.size
    half = RET_HEAD_DIM // 2
    inv_freq = ROPE_BASE ** (-jnp.arange(half, dtype=F32) / half)
    invf = jnp.concatenate([inv_freq, inv_freq])[None, :]
    sign = jnp.concatenate([-jnp.ones((half,), F32), jnp.ones((half,), F32)])[None, :]
    pos = jnp.broadcast_to(positions.astype(F32).reshape(n_tok, 1), (n_tok, RET_HEAD_DIM))
    tm = 1024
    row = pl.BlockSpec((tm, RET_HEAD_DIM), lambda i: (i, 0))
    const = pl.BlockSpec((1, RET_HEAD_DIM), lambda i: (0, 0))
    return pl.pallas_call(
        _rope_kernel,
        grid=(n_tok // tm,),
        in_specs=[row, const, const],
        out_specs=[row, row],
        out_shape=[jax.ShapeDtypeStruct((n_tok, RET_HEAD_DIM), F32)] * 2,
        compiler_params=_params(("arbitrary",)),
        name="rope",
    )(pos, invf, sign)


def _norm_mod(x, g, shift, scale):
    ms = jnp.mean(x * x, axis=-1, keepdims=True)
    return (x * lax.rsqrt(ms + NORM_EPS) * g) * (1.0 + scale) + shift


def _proj_kernel(x_ref, g_ref, shift_ref, scale_ref, w_ref, b_ref, o_ref, h_ref, *, n_gate):
    j = pl.program_id(2)

    @pl.when(j == 0)
    def _():
        h_ref[...] = _norm_mod(x_ref[...], g_ref[...], shift_ref[0], scale_ref[0]).astype(BF16)

    acc = _mm(h_ref[...], w_ref[...])

    @pl.when(j < n_gate)
    def _():
        o_ref[...] = jax.nn.sigmoid(acc + b_ref[...]).astype(o_ref.dtype)

    @pl.when(j >= n_gate)
    def _():
        o_ref[...] = acc.astype(o_ref.dtype)


def _proj(x, norm_g, shift, scale, w_cat, b_cat, bsz, seq):
    d = x.shape[-1]
    tm = min(PROJ_TM, seq)
    nt = seq // tm
    return pl.pallas_call(
        functools.partial(_proj_kernel, n_gate=GATE_COLS // PROJ_TN),
        grid=(bsz, nt, PROJ_COLS // PROJ_TN),
        in_specs=[pl.BlockSpec((tm, d), lambda b, i, j: (b * nt + i, 0)),
                  pl.BlockSpec((1, d), lambda b, i, j: (0, 0)),
                  pl.BlockSpec((1, 1, d), lambda b, i, j: (b, 0, 0)),
                  pl.BlockSpec((1, 1, d), lambda b, i, j: (b, 0, 0)),
                  pl.BlockSpec((d, PROJ_TN), lambda b, i, j: (0, j)),
                  pl.BlockSpec((1, PROJ_TN), lambda b, i, j: (0, j))],
        out_specs=pl.BlockSpec((tm, PROJ_TN), lambda b, i, j: (b * nt + i, j)),
        out_shape=jax.ShapeDtypeStruct((bsz * seq, PROJ_COLS), BF16),
        scratch_shapes=[pltpu.VMEM((tm, d), BF16)],
        compiler_params=_params(("arbitrary", "arbitrary", "arbitrary")),
        name="proj",
    )(x, norm_g, shift, scale, w_cat, b_cat)


def _ret_tables():
    log_gamma = jnp.log1p(-jnp.exp2(-5.0 - jnp.arange(RET_HEADS, dtype=F32)))
    pos = jnp.arange(RET_CHUNK, dtype=F32)
    rel = pos[:, None] - pos[None, :]
    decay_in = jnp.where(rel >= 0, jnp.exp(log_gamma[:, None, None] * jnp.maximum(rel, 0.0)), 0.0)
    decay_k = jnp.exp(log_gamma[:, None] * (RET_CHUNK - 1 - pos))
    decay_q = jnp.exp(log_gamma[:, None] * (pos + 1.0))
    decay_chunk = jnp.exp(log_gamma * RET_CHUNK)
    bcast = lambda t: jnp.broadcast_to(t[:, :, None], (RET_HEADS, RET_CHUNK, RET_HEAD_DIM))
    dc = jnp.broadcast_to(decay_chunk[:, None, None], (RET_HEADS, 1, RET_HEAD_DIM))
    return decay_in, bcast(decay_k), bcast(decay_q), dc


def _ret_kernel(q_ref, k_ref, v_ref, g_ref, cos_ref, sin_ref, din_ref, zeta_ref, xi_ref, dc_ref, gn_ref,
                o_ref, state_ref, *, tb):
    @pl.when(pl.program_id(1) == 0)
    def _():
        state_ref[...] = jnp.zeros_like(state_ref)

    k_scale = RET_HEAD_DIM ** -0.5
    for c in range(tb // RET_CHUNK):
        rows = slice(c * RET_CHUNK, (c + 1) * RET_CHUNK)
        cosf = cos_ref[rows, :]
        sinf = sin_ref[rows, :]
        for h in range(RET_HEADS):
            cols = slice(h * RET_HEAD_DIM, (h + 1) * RET_HEAD_DIM)
            q = q_ref[rows, cols].astype(F32)
            k = k_ref[rows, cols].astype(F32)
            qr = q * cosf + pltpu.roll(q, RET_HEAD_DIM // 2, 1) * sinf
            kr = (k * cosf + pltpu.roll(k, RET_HEAD_DIM // 2, 1) * sinf) * k_scale
            qb = qr.astype(BF16)
            vb = v_ref[rows, cols].astype(BF16)
            scores = _nt(qb, kr.astype(BF16)) * din_ref[h]
            inner = _mm(scores.astype(BF16), vb)
            st = state_ref[h]
            cross = _mm(qb, st.astype(BF16)) * xi_ref[h]
            state_ref[h] = st * dc_ref[h] + _tn((kr * zeta_ref[h]).astype(BF16), vb)
            out = inner + cross
            mu = jnp.mean(out, axis=-1, keepdims=True)
            dev = out - mu
            var = jnp.mean(dev * dev, axis=-1, keepdims=True)
            g = g_ref[rows, cols].astype(F32)
            y = (g * jax.nn.sigmoid(g)) * (dev * lax.rsqrt(var + HEAD_NORM_EPS) * gn_ref[:, cols])
            o_ref[rows, cols] = y.astype(o_ref.dtype)


def _retention(pg, cosf, sinf, tables, gn_g, bsz, seq):
    tb = min(RET_TB, seq)
    nt = seq // tb
    mw = MIX_WIDTH
    col = lambda cb: pl.BlockSpec((tb, mw), lambda b, i, cb=cb: (b * nt + i, cb))
    rope = pl.BlockSpec((tb, RET_HEAD_DIM), lambda b, i: (b * nt + i, 0))
    tab = pl.BlockSpec((RET_HEADS, RET_CHUNK, RET_HEAD_DIM), lambda b, i: (0, 0, 0))
    c0 = COL_RET // mw
    return pl.pallas_call(
        functools.partial(_ret_kernel, tb=tb),
        grid=(bsz, nt),
        in_specs=[col(c0), col(c0 + 1), col(c0 + 2), col(c0 + 3), rope, rope, tab, tab, tab,
                  pl.BlockSpec((RET_HEADS, 1, RET_HEAD_DIM), lambda b, i: (0, 0, 0)),
                  pl.BlockSpec((1, mw), lambda b, i: (0, 0))],
        out_specs=pl.BlockSpec((tb, mw), lambda b, i: (b * nt + i, 0)),
        out_shape=jax.ShapeDtypeStruct((bsz * seq, mw), BF16),
        scratch_shapes=[pltpu.VMEM((RET_HEADS, RET_HEAD_DIM, RET_HEAD_DIM), F32)],
        compiler_params=_params(("arbitrary", "arbitrary")),
        name="ret",
    )(pg, pg, pg, pg, cosf, sinf, *tables, gn_g)


def _rwkv_kernel(r_ref, k_ref, v_ref, lo_ref, mur_ref, muk_ref, muv_ref, mulo_ref, w0_ref, a0_ref,
                 wl_ref, al_ref, gl_ref, kk_ref, ka_ref, rk_ref, gn_ref, tri_ref, ones_ref,
                 o_ref, cr, ck, cv, clo, state, s_rt, s_at, s_kt, s_bt, s_kw, s_bw, s_v, s_g, s_bonus, s_y,
                 *, tb):
    @pl.when(pl.program_id(1) == 0)
    def _():
        for ref in (cr, ck, cv, clo, state):
            ref[...] = jnp.zeros_like(ref)

    def shift(ref, carry, mu_ref):
        p = ref[...].astype(F32)
        row = lax.broadcasted_iota(jnp.int32, p.shape, 0)
        prev = jnp.where(row == 0, carry[SUBLANES - 1:SUBLANES, :], pltpu.roll(p, 1, 0))
        carry[...] = p[tb - SUBLANES:tb, :]
        return p + mu_ref[...] * (prev - p)

    r = shift(r_ref, cr, mur_ref)
    k = shift(k_ref, ck, muk_ref)
    v = shift(v_ref, cv, muv_ref)
    lo = shift(lo_ref, clo, mulo_ref)
    wd = lo[:, 0:DECAY_LORA]
    ad = lo[:, DECAY_LORA:DECAY_LORA + AAA_LORA]
    gd = lo[:, DECAY_LORA + AAA_LORA:LORA_COLS]
    logw = -math.exp(-0.5) * jax.nn.sigmoid(w0_ref[...] + _mm(jnp.tanh(wd).astype(BF16), wl_ref[...]))
    a = jax.nn.sigmoid(a0_ref[...] + _mm(ad.astype(BF16), al_ref[...]))
    s_g[...] = _mm(jax.nn.sigmoid(gd).astype(BF16), gl_ref[...])
    kk = k * kk_ref[...]
    kh = kk * lax.rsqrt(jnp.maximum(_mm((kk * kk).astype(BF16), ones_ref[...]), 1e-12))
    kt = k * (1.0 + (a - 1.0) * ka_ref[...])
    s_bonus[...] = _mm((r * kt * rk_ref[...]).astype(BF16), ones_ref[...]) * v
    ch = RWKV_CHUNK
    hd = RWKV_HEAD_DIM
    n_ch = tb // ch
    mw = MIX_WIDTH
    b = a * kh
    cum = _mm_split_left(tri_ref[...], logw)
    ends = [cum[(c + 1) * ch - 1:(c + 1) * ch, :] for c in range(n_ch)]
    cend = jnp.concatenate([jnp.broadcast_to(e, (ch, mw)) for e in ends], axis=0)
    e_inv = jnp.exp(-cum)
    e_end = jnp.exp(cend - cum)
    s_rt[...] = r * jnp.exp(cum)
    s_at[...] = kh * jnp.exp(cum - logw)
    s_kt[...] = kt * e_inv
    s_bt[...] = b * e_inv
    s_kw[...] = kt * e_end
    s_bw[...] = b * e_end
    s_v[...] = v

    lane = lax.broadcasted_iota(jnp.int32, (ch, LANES), 1)
    rowi = lax.broadcasted_iota(jnp.int32, (ch, LANES), 0)
    strict = (rowi > (lane % hd)).astype(F32)
    incl = (rowi >= (lane % hd)).astype(F32)
    lane2 = lax.broadcasted_iota(jnp.int32, (2 * ch, LANES), 1)
    row2 = lax.broadcasted_iota(jnp.int32, (2 * ch, LANES), 0)
    first2 = lane2 < hd
    blockdiag = ((row2 < hd) == (lane2 < hd)).astype(F32)
    zeros_sq = jnp.zeros((2 * ch, LANES), BF16)

    def bd(x, swap):
        lane_x = lax.broadcasted_iota(jnp.int32, x.shape, 1)
        first = (lane_x % LANES) < hd
        top = jnp.where(first, 0.0, x) if swap else jnp.where(first, x, 0.0)
        bot = jnp.where(first, x, 0.0) if swap else jnp.where(first, 0.0, x)
        return jnp.concatenate([top, bot], axis=0).astype(BF16)

    items = [(c, p) for c in range(n_ch) for p in range(RWKV_PAIRS)]
    rows = lambda c: slice(c * ch, (c + 1) * ch)
    cols = lambda p: slice(p * LANES, (p + 1) * LANES)
    blk = lambda ref, i: ref[rows(i[0]), cols(i[1])]
    a_t = [blk(s_at, i) for i in items]
    r_t = [blk(s_rt, i) for i in items]
    k_t = [blk(s_kt, i) for i in items]
    b_t = [blk(s_bt, i) for i in items]
    n_it = range(len(items))
    ar = [jnp.concatenate([a_t[i], r_t[i]], axis=0) for i in n_it]
    kb = [jnp.concatenate([k_t[i], b_t[i]], axis=0).astype(BF16) for i in n_it]
    bk = [jnp.concatenate([b_t[i], k_t[i]], axis=0).astype(BF16) for i in n_it]
    m_a = [_nt(jnp.where(first2, ar[i], 0.0).astype(BF16), kb[i]) for i in n_it]
    m_b = [_nt(jnp.where(first2, 0.0, ar[i]).astype(BF16), bk[i]) for i in n_it]
    k_sc = [jnp.where(first2, m_a[i], m_b[i]) for i in n_it]
    b_sc = [jnp.where(first2, m_b[i], m_a[i]) for i in n_it]
    a_ak = [k_sc[i][0:ch] * strict for i in n_it]
    n_ab = [b_sc[i][0:ch] * strict for i in n_it]
    a_rk = [k_sc[i][ch:2 * ch] * incl for i in n_it]
    a_rb = [b_sc[i][ch:2 * ch] * incl for i in n_it]
    v2 = [blk(s_v, i) for i in items]
    v_bd = [bd(v2[i], False) for i in n_it]
    x = [jnp.concatenate([_mm(a_ak[i].astype(BF16), v_bd[i]), a_t[i]], axis=1) for i in n_it]
    eye = (rowi == (lane % hd)).astype(F32)
    m = [-n_ab[i] for i in n_it]
    t = [eye + m[i] for i in n_it]
    m = [_mm(m[i].astype(BF16), bd(m[i], False)) for i in n_it]
    for j in range(1, 6):
        if j < 5:
            res = [_mm(m[i].astype(BF16), jnp.concatenate([bd(t[i], False), bd(m[i], False)], axis=1))
                   for i in n_it]
            t = [t[i] + res[i][:, 0:LANES] for i in n_it]
            m = [res[i][:, LANES:2 * LANES] for i in n_it]
        else:
            t = [t[i] + _mm(m[i].astype(BF16), bd(t[i], False)) for i in n_it]
    x = [_mm(t[i].astype(BF16), bd(x[i], True)) for i in n_it]
    u_loc = [x[i][:, 0:LANES] for i in n_it]
    z = [x[i][:, LANES:2 * LANES] for i in n_it]
    yg = [_mm(jnp.concatenate([a_rk[i], -a_rb[i]], axis=1).astype(BF16),
              jnp.concatenate([jnp.concatenate([v_bd[i], zeros_sq], axis=1), bd(x[i], True)], axis=0))
          for i in n_it]
    y_loc = [yg[i][:, 0:LANES] for i in n_it]
    g = [(r_t[i] + yg[i][:, LANES:2 * LANES]).astype(BF16) for i in n_it]
    q = [_tn(jnp.concatenate([v2[i], u_loc[i]], axis=0).astype(BF16),
             jnp.concatenate([blk(s_kw, items[i]), -blk(s_bw, items[i])], axis=0).astype(BF16)) * blockdiag
         for i in n_it]
    pm = [(_tn(z[i].astype(BF16), blk(s_bw, items[i]).astype(BF16)) * blockdiag).astype(BF16) for i in n_it]

    for p in range(RWKV_PAIRS):
        st = state[p]
        for c in range(n_ch):
            i = c * RWKV_PAIRS + p
            sb = st.astype(BF16)
            s_y[rows(c), cols(p)] = y_loc[i] + _nt(g[i], sb)
            st = st * jnp.exp(ends[c][:, cols(p)]) - _mm(sb, pm[i]) + q[i]
        state[p] = st

    y = s_y[...]
    inv_n = 1.0 / hd
    mu = _mm_split(y, ones_ref[...]) * inv_n
    dev = y - mu
    var = _mm((dev * dev).astype(BF16), ones_ref[...]) * inv_n
    yn = dev * lax.rsqrt(var + RWKV_NORM_EPS) * gn_ref[...]
    o_ref[...] = ((yn + s_bonus[...]) * s_g[...]).astype(o_ref.dtype)


def _rwkv(pg, mu, w0, w_lora, a0, a_lora, g_lora, k_k, k_a, r_k, gn_g, bsz, seq):
    tb = min(RWKV_TB, seq)
    nt = seq // tb
    mw = MIX_WIDTH
    col = lambda cb: pl.BlockSpec((tb, mw), lambda b, i, cb=cb: (b * nt + i, cb))
    c0 = COL_RWKV // mw
    const = lambda shape: pl.BlockSpec(shape, lambda b, i: (0,) * len(shape))
    row = lambda t: t.reshape(1, -1)
    idx = jnp.arange(tb)
    same_chunk = (idx[:, None] // RWKV_CHUNK) == (idx[None, :] // RWKV_CHUNK)
    tri = ((idx[:, None] >= idx[None, :]) & same_chunk).astype(BF16)
    hid = jnp.arange(mw) // RWKV_HEAD_DIM
    ones = (hid[:, None] == hid[None, :]).astype(BF16)
    vec = const((1, mw))
    buf = pltpu.VMEM((tb, mw), F32)
    return pl.pallas_call(
        functools.partial(_rwkv_kernel, tb=tb),
        grid=(bsz, nt),
        in_specs=[col(c0), col(c0 + 1), col(c0 + 2),
                  pl.BlockSpec((tb, LORA_COLS), lambda b, i: (b * nt + i, COL_LORA // LORA_COLS)),
                  vec, vec, vec, const((1, LORA_COLS)), vec, vec,
                  const((DECAY_LORA, mw)), const((AAA_LORA, mw)), const((GATE_LORA, mw)),
                  vec, vec, vec, vec, const((tb, tb)), const((mw, mw))],
        out_specs=pl.BlockSpec((tb, mw), lambda b, i: (b * nt + i, 0)),
        out_shape=jax.ShapeDtypeStruct((bsz * seq, mw), BF16),
        scratch_shapes=[pltpu.VMEM((SUBLANES, mw), F32)] * 3 + [pltpu.VMEM((SUBLANES, LORA_COLS), F32),
                        pltpu.VMEM((RWKV_PAIRS, LANES, LANES), F32)] + [buf] * 10,
        compiler_params=_params(("arbitrary", "arbitrary")),
        name="rwkv",
    )(pg, pg, pg, pg, row(mu[0:mw]), row(mu[mw:2 * mw]), row(mu[2 * mw:3 * mw]), row(mu[3 * mw:]),
      row(w0), row(a0), w_lora.astype(BF16), a_lora.astype(BF16), g_lora.astype(BF16),
      row(k_k), row(k_a), row(r_k), row(gn_g), tri, ones)


def _merge_kernel(ch_ref, cb_ref, cc_ref, g0_ref, g1_ref, g2_ref, yret_ref, yrwkv_ref, x_ref, convw_ref,
                  wb_ref, wo_ref, gate1_ref, n2g_ref, shift2_ref, scale2_ref, wr_ref, br_ref, tri_ref,
                  x1_ref, h2_ref, route_ref, cnt_ref, ucarry, cnt_scr, *, tm):
    first_step = (pl.program_id(0) == 0) & (pl.program_id(1) == 0)

    @pl.when(first_step)
    def _():
        cnt_scr[...] = jnp.zeros_like(cnt_scr)

    @pl.when(pl.program_id(1) == 0)
    def _():
        ucarry[...] = jnp.zeros_like(ucarry)

    u = cc_ref[...].astype(F32) * ch_ref[...].astype(F32)
    row = lax.broadcasted_iota(jnp.int32, u.shape, 0)
    last = ucarry[SUBLANES - 1:SUBLANES, :]
    u1 = jnp.where(row == 0, last, pltpu.roll(u, 1, 0))
    u2 = jnp.where(row == 0, ucarry[SUBLANES - 2:SUBLANES - 1, :], jnp.where(row == 1, last, pltpu.roll(u, 2, 0)))
    ucarry[...] = u[tm - SUBLANES:tm, :]
    y_conv = cb_ref[...].astype(F32) * (convw_ref[0:1, :] * u2 + convw_ref[1:2, :] * u1 + convw_ref[2:3, :] * u)

    merged = g0_ref[...].astype(F32) * _mm(yret_ref[...], wb_ref[0])
    merged = merged + g1_ref[...].astype(F32) * _mm(yrwkv_ref[...], wb_ref[1])
    merged = merged + g2_ref[...].astype(F32) * _mm(y_conv.astype(BF16), wb_ref[2])
    x1 = x_ref[...] + gate1_ref[0] * _mm(merged.astype(BF16), wo_ref[...])
    x1_ref[...] = x1
    h2 = _norm_mod(x1, n2g_ref[...], shift2_ref[0], scale2_ref[0])
    bits = lax.bitcast_convert_type(h2.astype(BF16).astype(F32), jnp.uint32)
    half = bits.shape[1] // 2
    h2_ref[...] = (bits[:, :half] >> 16) | (bits[:, half:] & jnp.uint32(0xFFFF0000))

    h_hi = h2.astype(BF16)
    h_lo = (h2 - h_hi.astype(F32)).astype(BF16)
    logits = _mm(h_hi, wr_ref[0]) + _mm(h_lo, wr_ref[0]) + _mm(h_hi, wr_ref[1]) + br_ref[...]
    lane = lax.broadcasted_iota(jnp.int32, logits.shape, 1)
    lanef = lane.astype(F32)
    neg = -jnp.inf
    big = 1e9
    is_group = lane < N_GROUPS
    gl = jnp.where(is_group, logits, neg)
    gmax = jnp.max(gl, axis=-1, keepdims=True)
    gsel = jnp.min(jnp.where(gl == gmax, lanef, big), axis=-1, keepdims=True)
    p_group = 1.0 / jnp.sum(jnp.where(is_group, jnp.exp(gl - gmax), 0.0), axis=-1, keepdims=True)
    lo = ROUTE_LANE0 + EXPERTS_PER_GROUP * gsel
    el = jnp.where((lanef >= lo) & (lanef < lo + EXPERTS_PER_GROUP), logits, neg)
    m1 = jnp.max(el, axis=-1, keepdims=True)
    i1 = jnp.min(jnp.where(el == m1, lanef, big), axis=-1, keepdims=True)
    el2 = jnp.where(lanef == i1, neg, el)
    m2 = jnp.max(el2, axis=-1, keepdims=True)
    i2 = jnp.min(jnp.where(el2 == m2, lanef, big), axis=-1, keepdims=True)
    t = jnp.exp(m2 - m1)
    w1 = p_group / (1.0 + t)
    w2 = p_group * t / (1.0 + t)

    hit1 = lanef == i1
    hit2 = lanef == i2
    onehot = jnp.where(hit1 | hit2, 1.0, 0.0)
    before = _mm(tri_ref[...], onehot.astype(BF16)) + cnt_scr[0:1, :]
    rank1 = jnp.sum(jnp.where(hit1, before, 0.0), axis=-1, keepdims=True)
    rank2 = jnp.sum(jnp.where(hit2, before, 0.0), axis=-1, keepdims=True)
    cnt_scr[...] = cnt_scr[...] + jnp.sum(onehot, axis=0, keepdims=True)
    cnt_ref[...] = cnt_scr[...]

    out = jnp.where(lane == 0, i1 - ROUTE_LANE0, 0.0)
    out = jnp.where(lane == 1, i2 - ROUTE_LANE0, out)
    out = jnp.where(lane == 2, w1, out)
    out = jnp.where(lane == 3, w2, out)
    out = jnp.where(lane == 4, rank1, out)
    out = jnp.where(lane == 5, rank2, out)
    route_ref[...] = out


def _merge(pg, y_ret, y_rwkv, x, conv_w, w_branch, w_o, gate1, norm2_g, shift2, scale2, w_route, b_route,
           bsz, seq):
    tm = min(MERGE_TM, seq)
    nt = seq // tm
    d = D_MODEL
    mw = MIX_WIDTH
    n_tok = bsz * seq
    tok = lambda width, cb=0: pl.BlockSpec((tm, width), lambda b, i, cb=cb: (b * nt + i, cb))
    const = lambda shape: pl.BlockSpec(shape, lambda b, i: (0,) * len(shape))
    per_b = pl.BlockSpec((1, 1, d), lambda b, i: (b, 0, 0))
    cc0 = COL_CONV // mw
    gc0 = COL_GATE // d
    idx = jnp.arange(tm)
    tri = (idx[:, None] > idx[None, :]).astype(BF16)
    convw = jnp.zeros((SUBLANES, mw), F32).at[0:conv_w.shape[0]].set(conv_w)
    return pl.pallas_call(
        functools.partial(_merge_kernel, tm=tm),
        grid=(bsz, nt),
        in_specs=[tok(mw, cc0), tok(mw, cc0 + 1), tok(mw, cc0 + 2),
                  tok(d, gc0), tok(d, gc0 + 1), tok(d, gc0 + 2),
                  tok(mw), tok(mw), tok(d), const((SUBLANES, mw)),
                  const((N_BRANCHES, mw, d)), const((d, d)), per_b, const((1, d)), per_b, per_b,
                  const((2, d, LANES)), const((1, LANES)), const((tm, tm))],
        out_specs=[tok(d), tok(d // 2), tok(LANES), const((SUBLANES, LANES))],
        out_shape=[jax.ShapeDtypeStruct((n_tok, d), F32), jax.ShapeDtypeStruct((n_tok, d // 2), jnp.uint32),
                   jax.ShapeDtypeStruct((n_tok, LANES), F32), jax.ShapeDtypeStruct((SUBLANES, LANES), F32)],
        scratch_shapes=[pltpu.VMEM((SUBLANES, mw), F32), pltpu.VMEM((SUBLANES, LANES), F32)],
        compiler_params=_params(("arbitrary", "arbitrary")),
        name="merge",
    )(pg, pg, pg, pg, pg, pg, y_ret, y_rwkv, x, convw, w_branch, w_o, gate1, norm2_g, shift2, scale2,
      jnp.stack([w_route.astype(BF16), (w_route - w_route.astype(BF16).astype(F32)).astype(BF16)]), b_route, tri)


def _row_copy(src, src_row, dst, dst_row, sem):
    return pltpu.make_async_copy(src.at[pl.ds(src_row, 1), :], dst.at[pl.ds(dst_row, 1), :], sem)


def _dispatch_kernel(dest_ref, h2_ref, rows_in_ref, rows_ref, sem, *, tile):
    del rows_in_ref

    def wait(r, carry):
        for k in range(2):
            _row_copy(h2_ref, 0, rows_ref, 0, sem).wait()
        return carry

    for r in range(tile):
        for k in range(2):
            _row_copy(h2_ref, r, rows_ref, dest_ref[0, 0, 2 * r + k], sem).start(priority=k)
    lax.fori_loop(0, tile, wait, 0, unroll=8)


def _dispatch(dest, h2, n_rows):
    n_tok, d = h2.shape
    tile = min(ROW_TILE, n_tok)
    nt = n_tok // tile
    zeros = jnp.zeros((n_rows, d), h2.dtype)
    return pl.pallas_call(
        functools.partial(_dispatch_kernel, tile=tile),
        grid=(nt,),
        in_specs=[pl.BlockSpec((1, 1, 2 * tile), lambda i: (i, 0, 0), memory_space=pltpu.SMEM),
                  pl.BlockSpec((tile, d), lambda i: (i, 0)),
                  pl.BlockSpec(memory_space=pl.ANY)],
        out_specs=pl.BlockSpec(memory_space=pl.ANY),
        out_shape=jax.ShapeDtypeStruct((n_rows, d), h2.dtype),
        scratch_shapes=[pltpu.SemaphoreType.DMA(())],
        input_output_aliases={2: 0},
        compiler_params=_params(("arbitrary",)),
        name="dispatch",
    )(dest.reshape(nt, 1, 2 * tile), h2, zeros)


def _expert_kernel(be_ref, nused_ref, x_ref, wg_ref, wu_ref, wd_ref, y_ref, wg_s, wu_s, wd_s):
    b = pl.program_id(0)
    used = b < nused_ref[0]

    @pl.when((b == 0) | (be_ref[b] != be_ref[jnp.maximum(b - 1, 0)]))
    def _():
        wg_s[...] = wg_ref[0].astype(BF16)
        wu_s[...] = wu_ref[0].astype(BF16)
        wd_s[...] = wd_ref[0].astype(BF16)

    @pl.when(used)
    def _():
        w = x_ref[...]
        xb = jnp.concatenate([lax.bitcast_convert_type(w << 16, F32),
                              lax.bitcast_convert_type(w & jnp.uint32(0xFFFF0000), F32)], axis=1).astype(BF16)
        g = _mm(xb, wg_s[...])
        u = _mm(xb, wu_s[...])
        y_ref[...] = _mm(((g * jax.nn.sigmoid(g)) * u).astype(BF16), wd_s[...])

    @pl.when(jnp.logical_not(used))
    def _():
        y_ref[...] = jnp.zeros_like(y_ref)


def _experts(block_e, n_used, x_rows, w_gate, w_up, w_down):
    n_rows = x_rows.shape[0]
    d, ff = w_gate.shape[-2:]
    grid_spec = pltpu.PrefetchScalarGridSpec(
        num_scalar_prefetch=2,
        grid=(n_rows // EXPERT_ROWS,),
        in_specs=[pl.BlockSpec((EXPERT_ROWS, d // 2), lambda b, be, nu: (b, 0)),
                  pl.BlockSpec((1, d, ff), lambda b, be, nu: (be[b], 0, 0)),
                  pl.BlockSpec((1, d, ff), lambda b, be, nu: (be[b], 0, 0)),
                  pl.BlockSpec((1, ff, d), lambda b, be, nu: (be[b], 0, 0))],
        out_specs=pl.BlockSpec((EXPERT_ROWS, d), lambda b, be, nu: (b, 0)),
        scratch_shapes=[pltpu.VMEM((d, ff), BF16), pltpu.VMEM((d, ff), BF16), pltpu.VMEM((ff, d), BF16)],
    )
    return pl.pallas_call(
        _expert_kernel,
        grid_spec=grid_spec,
        out_shape=jax.ShapeDtypeStruct((n_rows, d), F32),
        compiler_params=_params(("arbitrary",)),
        name="expert",
    )(block_e, n_used, x_rows, w_gate, w_up, w_down)


def _combine_kernel(dest_ref, dest_next_ref, x1_ref, route_ref, gate2_ref, fg_ref, rows_ref, o_ref, ybuf, sem,
                    *, tile, final, n_tiles):
    i = pl.program_id(0)
    slot = i % 2

    def issue(dref, s):
        for r in range(tile):
            for k in range(2):
                _row_copy(rows_ref, dref[0, 0, 2 * r + k], ybuf.at[s, k], r, sem.at[s]).start(priority=k)

    @pl.when(i == 0)
    def _():
        issue(dest_ref, 0)

    for s in range(2):
        @pl.when((i + 1 < n_tiles) & (slot == 1 - s))
        def _(s=s):
            issue(dest_next_ref, s)

    def wait(r, carry):
        for k in range(2):
            _row_copy(rows_ref, 0, ybuf.at[slot, k], 0, sem.at[slot]).wait()
        return carry

    lax.fori_loop(0, tile, wait, 0, unroll=8)
    route = route_ref[...]
    y = route[:, 2:3] * ybuf[slot, 0] + route[:, 3:4] * ybuf[slot, 1]
    x2 = x1_ref[...] + gate2_ref[0] * y
    if final:
        ms = jnp.mean(x2 * x2, axis=-1, keepdims=True)
        x2 = x2 * lax.rsqrt(ms + NORM_EPS) * fg_ref[...]
    o_ref[...] = x2


def _combine(dest, x1, route, gate2, final_g, y_rows, seq, final):
    n_tok, d = x1.shape
    tile = min(ROW_TILE, seq)
    nt = n_tok // tile
    per_seq = seq // tile
    return pl.pallas_call(
        functools.partial(_combine_kernel, tile=tile, final=final, n_tiles=nt),
        grid=(nt,),
        in_specs=[pl.BlockSpec((1, 1, 2 * tile), lambda i: (i, 0, 0), memory_space=pltpu.SMEM),
                  pl.BlockSpec((1, 1, 2 * tile), lambda i: (jnp.minimum(i + 1, nt - 1), 0, 0),
                               memory_space=pltpu.SMEM),
                  pl.BlockSpec((tile, d), lambda i: (i, 0)),
                  pl.BlockSpec((tile, LANES), lambda i: (i, 0)),
                  pl.BlockSpec((1, 1, d), lambda i: (i // per_seq, 0, 0)),
                  pl.BlockSpec((1, d), lambda i: (0, 0)),
                  pl.BlockSpec(memory_space=pl.ANY)],
        out_specs=pl.BlockSpec((tile, d), lambda i: (i, 0)),
        out_shape=jax.ShapeDtypeStruct((n_tok, d), F32),
        scratch_shapes=[pltpu.VMEM((2, 2, tile, d), F32), pltpu.SemaphoreType.DMA((2,))],
        compiler_params=_params(("arbitrary",)),
        name="combine",
    )(dest.reshape(nt, 1, 2 * tile), dest.reshape(nt, 1, 2 * tile), x1, route, gate2, final_g, y_rows)


def _moe(x1, h2, route, counts, gate2, final_g, w_gate, w_up, w_down, seq, final):
    n_tok = x1.shape[0]
    n_rows = 2 * n_tok + N_EXPERTS * EXPERT_ROWS
    n_blocks = n_rows // EXPERT_ROWS
    cnt = counts[0, ROUTE_LANE0:ROUTE_LANE0 + N_EXPERTS].astype(jnp.int32)
    padded = (cnt + EXPERT_ROWS - 1) // EXPERT_ROWS * EXPERT_ROWS
    pends = jnp.cumsum(padded)
    pstarts = pends - padded
    eid = route[:, 0:2].astype(jnp.int32)
    dest = (pstarts[eid] + route[:, 4:6].astype(jnp.int32)).reshape(-1)
    n_used = (pends[-1] // EXPERT_ROWS).astype(jnp.int32)
    blk = jnp.minimum(jnp.arange(n_blocks, dtype=jnp.int32), n_used - 1)
    block_e = jnp.sum((pends[None, :] <= (blk * EXPERT_ROWS)[:, None]).astype(jnp.int32), axis=1)
    block_e = jnp.minimum(block_e, N_EXPERTS - 1)
    x_rows = _dispatch(dest, h2, n_rows)
    y_rows = _experts(block_e, n_used.reshape(1), x_rows, w_gate, w_up, w_down)
    return _combine(dest, x1, route, gate2, final_g, y_rows, seq, final)


def kernel(x, c, positions, norm1_g, norm2_g, final_norm_g, w_ada, b_ada, w_in, w_gate, b_gate, ret_gn_g, rwkv_mu, rwkv_w0, rwkv_w_lora, rwkv_a0, rwkv_a_lora, rwkv_g_lora, rwkv_k_k, rwkv_k_a, rwkv_r_k, rwkv_gn_g, conv_w, w_branch, w_o, w_router_group, b_router_group, w_router_expert, b_router_expert, w_exp_gate, w_exp_up, w_exp_down):
    bsz, seq, d = x.shape
    assert d == D_MODEL and seq % RET_TB == 0
    n_layers = w_in.shape[0]
    n_tok = bsz * seq
    xt = x.reshape(n_tok, d)
    mod = _ada(c, w_ada, b_ada)
    cosf, sinf = _rope_tables(positions)
    ret_tables = _ret_tables()
    for l in range(n_layers):
        shift1, scale1, gate1, shift2, scale2, gate2 = [
            mod[l, :, i * d:(i + 1) * d].reshape(bsz, 1, d) for i in range(6)]
        o_rwkv = RET_COLS
        o_conv = RET_COLS + RWKV_COLS
        w_cat = jnp.concatenate([w_gate[l], w_in[l][:, :RET_COLS], w_in[l][:, o_conv:], w_in[l][:, o_rwkv:o_conv]],
                                axis=1).astype(BF16)
        b_cat = jnp.concatenate([b_gate[l], jnp.zeros((IN_COLS,), F32)])[None, :]
        pg = _proj(xt, norm1_g[l][None, :], shift1, scale1, w_cat, b_cat, bsz, seq)
        y_ret = _retention(pg, cosf, sinf, ret_tables, ret_gn_g[l][None, :], bsz, seq)
        y_rwkv = _rwkv(pg, rwkv_mu[l], rwkv_w0[l], rwkv_w_lora[l], rwkv_a0[l], rwkv_a_lora[l], rwkv_g_lora[l],
                       rwkv_k_k[l], rwkv_k_a[l], rwkv_r_k[l].reshape(-1), rwkv_gn_g[l], bsz, seq)
        w_route = jnp.zeros((d, LANES), F32)
        w_route = w_route.at[:, 0:N_GROUPS].set(w_router_group[l])
        w_route = w_route.at[:, ROUTE_LANE0:ROUTE_LANE0 + N_EXPERTS].set(w_router_expert[l])
        b_route = jnp.zeros((1, LANES), F32)
        b_route = b_route.at[0, 0:N_GROUPS].set(b_router_group[l])
        b_route = b_route.at[0, ROUTE_LANE0:ROUTE_LANE0 + N_EXPERTS].set(b_router_expert[l])
        x1, h2, route, counts = _merge(pg, y_ret, y_rwkv, xt, conv_w[l], w_branch[l].astype(BF16),
                                       w_o[l].astype(BF16), gate1, norm2_g[l][None, :], shift2, scale2,
                                       w_route, b_route, bsz, seq)
        xt = _moe(x1, h2, route, counts, gate2, final_norm_g[None, :], w_exp_gate[l], w_exp_up[l], w_exp_down[l],
                  seq, final=(l == n_layers - 1))
    return xt.reshape(bsz, seq, d)
```

```python
import functools
import math

import jax
import jax.numpy as jnp
from jax import lax
from jax.experimental import pallas as pl
from jax.experimental.pallas import tpu as pltpu

F32 = jnp.float32
BF16 = jnp.bfloat16
HI = lax.Precision.HIGHEST

D_MODEL = 1024
MIX_WIDTH = D_MODEL // 2
RET_HEAD_DIM = 128
RET_HEADS = MIX_WIDTH // RET_HEAD_DIM
RET_CHUNK = 128
ROPE_BASE = 10000.0
RWKV_HEAD_DIM = 64
RWKV_HEADS = MIX_WIDTH // RWKV_HEAD_DIM
RWKV_PAIRS = RWKV_HEADS // 2
RWKV_CHUNK = 64
DECAY_LORA = 64
AAA_LORA = 64
GATE_LORA = 128
LORA_COLS = DECAY_LORA + AAA_LORA + GATE_LORA
N_BRANCHES = 3
RET_COLS = 4 * MIX_WIDTH
CONV_COLS = 3 * MIX_WIDTH
RWKV_COLS = 3 * MIX_WIDTH + LORA_COLS
IN_COLS = RET_COLS + RWKV_COLS + CONV_COLS
GATE_COLS = N_BRANCHES * D_MODEL
N_GROUPS = 4
EXPERTS_PER_GROUP = 8
N_EXPERTS = N_GROUPS * EXPERTS_PER_GROUP
EXPERT_FF = D_MODEL // 2
NORM_EPS = 1e-6
HEAD_NORM_EPS = 1e-5
RWKV_NORM_EPS = 64e-5

LANES = 128
SUBLANES = 8
VMEM_LIMIT = 56 * 1024 * 1024

COL_GATE = 0
COL_RET = GATE_COLS
COL_CONV = COL_RET + RET_COLS
COL_RWKV = COL_CONV + CONV_COLS
COL_LORA = COL_RWKV + 3 * MIX_WIDTH
PROJ_COLS = IN_COLS + GATE_COLS

PROJ_TM = 2048
PROJ_TN = 768
RET_TB = 512
RWKV_TB = 256
MERGE_TM = 256
ROW_TILE = 256
EXPERT_ROWS = 512
ROUTE_LANE0 = N_GROUPS


def _params(sem, vmem=VMEM_LIMIT):
    return pltpu.CompilerParams(dimension_semantics=sem, vmem_limit_bytes=vmem)


def _nt(a, b):
    return lax.dot_general(a, b, (((1,), (1,)), ((), ())), preferred_element_type=F32)


def _tn(a, b):
    return lax.dot_general(a, b, (((0,), (0,)), ((), ())), preferred_element_type=F32)


def _mm(a, b):
    return jnp.dot(a, b, preferred_element_type=F32)


def _mm_hi(a, b):
    return jnp.dot(a, b, precision=HI, preferred_element_type=F32)


def _mm_split(a, b01):
    hi = a.astype(BF16)
    lo = (a - hi.astype(F32)).astype(BF16)
    return _mm(hi, b01) + _mm(lo, b01)


def _mm_split_left(a01, b):
    hi = b.astype(BF16)
    lo = (b - hi.astype(F32)).astype(BF16)
    return _mm(a01, hi) + _mm(a01, lo)


def _ada_kernel(c_ref, w_ref, b_ref, o_ref):
    c = c_ref[...]
    s = c * jax.nn.sigmoid(c)
    o_ref[0] = _mm_hi(s, w_ref[0]) + b_ref[0]


def _ada(c, w_ada, b_ada):
    n_l, d, d6 = w_ada.shape
    bsz = c.shape[0]
    return pl.pallas_call(
        _ada_kernel,
        grid=(n_l, d6 // d),
        in_specs=[pl.BlockSpec((bsz, d), lambda l, j: (0, 0)),
                  pl.BlockSpec((1, d, d), lambda l, j: (l, 0, j)),
                  pl.BlockSpec((1, 1, d), lambda l, j: (l, 0, j))],
        out_specs=pl.BlockSpec((1, bsz, d), lambda l, j: (l, 0, j)),
        out_shape=jax.ShapeDtypeStruct((n_l, bsz, d6), F32),
        compiler_params=_params(("arbitrary", "arbitrary")),
        name="ada",
    )(c, w_ada, b_ada.reshape(n_l, 1, d6))


def _rope_kernel(pos_ref, invf_ref, sign_ref, cos_ref, sin_ref):
    ang = pos_ref[...] * invf_ref[...]
    cos_ref[...] = jnp.cos(ang)
    sin_ref[...] = jnp.sin(ang) * sign_ref[...]


def _rope_tables(positions):
    n_tok = positions.size
    half = RET_HEAD_DIM // 2
    inv_freq = ROPE_BASE ** (-jnp.arange(half, dtype=F32) / half)
    invf = jnp.concatenate([inv_freq, inv_freq])[None, :]
    sign = jnp.concatenate([-jnp.ones((half,), F32), jnp.ones((half,), F32)])[None, :]
    pos = jnp.broadcast_to(positions.astype(F32).reshape(n_tok, 1), (n_tok, RET_HEAD_DIM))
    tm = 1024
    row = pl.BlockSpec((tm, RET_HEAD_DIM), lambda i: (i, 0))
    const = pl.BlockSpec((1, RET_HEAD_DIM), lambda i: (0, 0))
    return pl.pallas_call(
        _rope_kernel,
        grid=(n_tok // tm,),
        in_specs=[row, const, const],
        out_specs=[row, row],
        out_shape=[jax.ShapeDtypeStruct((n_tok, RET_HEAD_DIM), F32)] * 2,
        compiler_params=_params(("arbitrary",)),
        name="rope",
    )(pos, invf, sign)


def _norm_mod(x, g, shift, scale):
    ms = jnp.mean(x * x, axis=-1, keepdims=True)
    return (x * lax.rsqrt(ms + NORM_EPS) * g) * (1.0 + scale) + shift


def _proj_kernel(x_ref, g_ref, shift_ref, scale_ref, w_ref, b_ref, o_ref, h_ref, *, n_gate):
    j = pl.program_id(2)

    @pl.when(j == 0)
    def _():
        h_ref[...] = _norm_mod(x_ref[...], g_ref[...], shift_ref[0], scale_ref[0]).astype(BF16)

    acc = _mm(h_ref[...], w_ref[...])

    @pl.when(j < n_gate)
    def _():
        o_ref[...] = jax.nn.sigmoid(acc + b_ref[...]).astype(o_ref.dtype)

    @pl.when(j >= n_gate)
    def _():
        o_ref[...] = acc.astype(o_ref.dtype)


def _proj(x, norm_g, shift, scale, w_cat, b_cat, bsz, seq):
    d = x.shape[-1]
    tm = min(PROJ_TM, seq)
    nt = seq // tm
    return pl.pallas_call(
        functools.partial(_proj_kernel, n_gate=GATE_COLS // PROJ_TN),
        grid=(bsz, nt, PROJ_COLS // PROJ_TN),
        in_specs=[pl.BlockSpec((tm, d), lambda b, i, j: (b * nt + i, 0)),
                  pl.BlockSpec((1, d), lambda b, i, j: (0, 0)),
                  pl.BlockSpec((1, 1, d), lambda b, i, j: (b, 0, 0)),
                  pl.BlockSpec((1, 1, d), lambda b, i, j: (b, 0, 0)),
                  pl.BlockSpec((d, PROJ_TN), lambda b, i, j: (0, j)),
                  pl.BlockSpec((1, PROJ_TN), lambda b, i, j: (0, j))],
        out_specs=pl.BlockSpec((tm, PROJ_TN), lambda b, i, j: (b * nt + i, j)),
        out_shape=jax.ShapeDtypeStruct((bsz * seq, PROJ_COLS), BF16),
        scratch_shapes=[pltpu.VMEM((tm, d), BF16)],
        compiler_params=_params(("arbitrary", "arbitrary", "arbitrary")),
        name="proj",
    )(x, norm_g, shift, scale, w_cat, b_cat)


def _ret_tables():
    log_gamma = jnp.log1p(-jnp.exp2(-5.0 - jnp.arange(RET_HEADS, dtype=F32)))
    pos = jnp.arange(RET_CHUNK, dtype=F32)
    rel = pos[:, None] - pos[None, :]
    decay_in = jnp.where(rel >= 0, jnp.exp(log_gamma[:, None, None] * jnp.maximum(rel, 0.0)), 0.0)
    decay_k = jnp.exp(log_gamma[:, None] * (RET_CHUNK - 1 - pos))
    decay_q = jnp.exp(log_gamma[:, None] * (pos + 1.0))
    decay_chunk = jnp.exp(log_gamma * RET_CHUNK)
    bcast = lambda t: jnp.broadcast_to(t[:, :, None], (RET_HEADS, RET_CHUNK, RET_HEAD_DIM))
    dc = jnp.broadcast_to(decay_chunk[:, None, None], (RET_HEADS, 1, RET_HEAD_DIM))
    return decay_in, bcast(decay_k), bcast(decay_q), dc


def _ret_kernel(q_ref, k_ref, v_ref, g_ref, cos_ref, sin_ref, din_ref, zeta_ref, xi_ref, dc_ref, gn_ref,
                o_ref, state_ref, *, tb):
    @pl.when(pl.program_id(1) == 0)
    def _():
        state_ref[...] = jnp.zeros_like(state_ref)

    k_scale = RET_HEAD_DIM ** -0.5
    for c in range(tb // RET_CHUNK):
        rows = slice(c * RET_CHUNK, (c + 1) * RET_CHUNK)
        cosf = cos_ref[rows, :]
        sinf = sin_ref[rows, :]
        for h in range(RET_HEADS):
            cols = slice(h * RET_HEAD_DIM, (h + 1) * RET_HEAD_DIM)
            q = q_ref[rows, cols].astype(F32)
            k = k_ref[rows, cols].astype(F32)
            qr = q * cosf + pltpu.roll(q, RET_HEAD_DIM // 2, 1) * sinf
            kr = (k * cosf + pltpu.roll(k, RET_HEAD_DIM // 2, 1) * sinf) * k_scale
            qb = qr.astype(BF16)
            vb = v_ref[rows, cols].astype(BF16)
            scores = _nt(qb, kr.astype(BF16)) * din_ref[h]
            inner = _mm(scores.astype(BF16), vb)
            st = state_ref[h]
            cross = _mm(qb, st.astype(BF16)) * xi_ref[h]
            state_ref[h] = st * dc_ref[h] + _tn((kr * zeta_ref[h]).astype(BF16), vb)
            out = inner + cross
            mu = jnp.mean(out, axis=-1, keepdims=True)
            dev = out - mu
            var = jnp.mean(dev * dev, axis=-1, keepdims=True)
            g = g_ref[rows, cols].astype(F32)
            y = (g * jax.nn.sigmoid(g)) * (dev * lax.rsqrt(var + HEAD_NORM_EPS) * gn_ref[:, cols])
            o_ref[rows, cols] = y.astype(o_ref.dtype)


def _retention(pg, cosf, sinf, tables, gn_g, bsz, seq):
    tb = min(RET_TB, seq)
    nt = seq // tb
    mw = MIX_WIDTH
    col = lambda cb: pl.BlockSpec((tb, mw), lambda b, i, cb=cb: (b * nt + i, cb))
    rope = pl.BlockSpec((tb, RET_HEAD_DIM), lambda b, i: (b * nt + i, 0))
    tab = pl.BlockSpec((RET_HEADS, RET_CHUNK, RET_HEAD_DIM), lambda b, i: (0, 0, 0))
    c0 = COL_RET // mw
    return pl.pallas_call(
        functools.partial(_ret_kernel, tb=tb),
        grid=(bsz, nt),
        in_specs=[col(c0), col(c0 + 1), col(c0 + 2), col(c0 + 3), rope, rope, tab, tab, tab,
                  pl.BlockSpec((RET_HEADS, 1, RET_HEAD_DIM), lambda b, i: (0, 0, 0)),
                  pl.BlockSpec((1, mw), lambda b, i: (0, 0))],
        out_specs=pl.BlockSpec((tb, mw), lambda b, i: (b * nt + i, 0)),
        out_shape=jax.ShapeDtypeStruct((bsz * seq, mw), BF16),
        scratch_shapes=[pltpu.VMEM((RET_HEADS, RET_HEAD_DIM, RET_HEAD_DIM), F32)],
        compiler_params=_params(("arbitrary", "arbitrary")),
        name="ret",
    )(pg, pg, pg, pg, cosf, sinf, *tables, gn_g)


def _rwkv_kernel(r_ref, k_ref, v_ref, lo_ref, mur_ref, muk_ref, muv_ref, mulo_ref, w0_ref, a0_ref,
                 wl_ref, al_ref, gl_ref, kk_ref, ka_ref, rk_ref, gn_ref, tri_ref, ones_ref,
                 o_ref, cr, ck, cv, clo, state, s_rt, s_at, s_kt, s_bt, s_kw, s_bw, s_v, s_g, s_bonus, s_y,
                 *, tb):
    @pl.when(pl.program_id(1) == 0)
    def _():
        for ref in (cr, ck, cv, clo, state):
            ref[...] = jnp.zeros_like(ref)

    def shift(ref, carry, mu_ref):
        p = ref[...].astype(F32)
        row = lax.broadcasted_iota(jnp.int32, p.shape, 0)
        prev = jnp.where(row == 0, carry[SUBLANES - 1:SUBLANES, :], pltpu.roll(p, 1, 0))
        carry[...] = p[tb - SUBLANES:tb, :]
        return p + mu_ref[...] * (prev - p)

    r = shift(r_ref, cr, mur_ref)
    k = shift(k_ref, ck, muk_ref)
    v = shift(v_ref, cv, muv_ref)
    lo = shift(lo_ref, clo, mulo_ref)
    wd = lo[:, 0:DECAY_LORA]
    ad = lo[:, DECAY_LORA:DECAY_LORA + AAA_LORA]
    gd = lo[:, DECAY_LORA + AAA_LORA:LORA_COLS]
    logw = -math.exp(-0.5) * jax.nn.sigmoid(w0_ref[...] + _mm(jnp.tanh(wd).astype(BF16), wl_ref[...]))
    a = jax.nn.sigmoid(a0_ref[...] + _mm(ad.astype(BF16), al_ref[...]))
    s_g[...] = _mm(jax.nn.sigmoid(gd).astype(BF16), gl_ref[...])
    kk = k * kk_ref[...]
    kh = kk * lax.rsqrt(jnp.maximum(_mm((kk * kk).astype(BF16), ones_ref[...]), 1e-12))
    kt = k * (1.0 + (a - 1.0) * ka_ref[...])
    s_bonus[...] = _mm((r * kt * rk_ref[...]).astype(BF16), ones_ref[...]) * v
    ch = RWKV_CHUNK
    hd = RWKV_HEAD_DIM
    n_ch = tb // ch
    mw = MIX_WIDTH
    b = a * kh
    cum = _mm_split_left(tri_ref[...], logw)
    ends = [cum[(c + 1) * ch - 1:(c + 1) * ch, :] for c in range(n_ch)]
    cend = jnp.concatenate([jnp.broadcast_to(e, (ch, mw)) for e in ends], axis=0)
    e_inv = jnp.exp(-cum)
    e_end = jnp.exp(cend - cum)
    s_rt[...] = r * jnp.exp(cum)
    s_at[...] = kh * jnp.exp(cum - logw)
    s_kt[...] = kt * e_inv
    s_bt[...] = b * e_inv
    s_kw[...] = kt * e_end
    s_bw[...] = b * e_end
    s_v[...] = v

    lane = lax.broadcasted_iota(jnp.int32, (ch, LANES), 1)
    rowi = lax.broadcasted_iota(jnp.int32, (ch, LANES), 0)
    strict = (rowi > (lane % hd)).astype(F32)
    incl = (rowi >= (lane % hd)).astype(F32)
    lane2 = lax.broadcasted_iota(jnp.int32, (2 * ch, LANES), 1)
    row2 = lax.broadcasted_iota(jnp.int32, (2 * ch, LANES), 0)
    first2 = lane2 < hd
    blockdiag = ((row2 < hd) == (lane2 < hd)).astype(F32)
    zeros_sq = jnp.zeros((2 * ch, LANES), BF16)

    def bd(x, swap):
        lane_x = lax.broadcasted_iota(jnp.int32, x.shape, 1)
        first = (lane_x % LANES) < hd
        top = jnp.where(first, 0.0, x) if swap else jnp.where(first, x, 0.0)
        bot = jnp.where(first, x, 0.0) if swap else jnp.where(first, 0.0, x)
        return jnp.concatenate([top, bot], axis=0).astype(BF16)

    items = [(c, p) for c in range(n_ch) for p in range(RWKV_PAIRS)]
    rows = lambda c: slice(c * ch, (c + 1) * ch)
    cols = lambda p: slice(p * LANES, (p + 1) * LANES)
    blk = lambda ref, i: ref[rows(i[0]), cols(i[1])]
    a_t = [blk(s_at, i) for i in items]
    r_t = [blk(s_rt, i) for i in items]
    k_t = [blk(s_kt, i) for i in items]
    b_t = [blk(s_bt, i) for i in items]
    n_it = range(len(items))
    ar = [jnp.concatenate([a_t[i], r_t[i]], axis=0) for i in n_it]
    kb = [jnp.concatenate([k_t[i], b_t[i]], axis=0).astype(BF16) for i in n_it]
    bk = [jnp.concatenate([b_t[i], k_t[i]], axis=0).astype(BF16) for i in n_it]
    m_a = [_nt(jnp.where(first2, ar[i], 0.0).astype(BF16), kb[i]) for i in n_it]
    m_b = [_nt(jnp.where(first2, 0.0, ar[i]).astype(BF16), bk[i]) for i in n_it]
    k_sc = [jnp.where(first2, m_a[i], m_b[i]) for i in n_it]
    b_sc = [jnp.where(first2, m_b[i], m_a[i]) for i in n_it]
    a_ak = [k_sc[i][0:ch] * strict for i in n_it]
    n_ab = [b_sc[i][0:ch] * strict for i in n_it]
    a_rk = [k_sc[i][ch:2 * ch] * incl for i in n_it]
    a_rb = [b_sc[i][ch:2 * ch] * incl for i in n_it]
    v2 = [blk(s_v, i) for i in items]
    v_bd = [bd(v2[i], False) for i in n_it]
    x = [jnp.concatenate([_mm(a_ak[i].astype(BF16), v_bd[i]), a_t[i]], axis=1) for i in n_it]
    eye = (rowi == (lane % hd)).astype(F32)
    m = [-n_ab[i] for i in n_it]
    t = [eye + m[i] for i in n_it]
    m = [_mm(m[i].astype(BF16), bd(m[i], False)) for i in n_it]
    for j in range(1, 6):
        if j < 5:
            res = [_mm(m[i].astype(BF16), jnp.concatenate([bd(t[i], False), bd(m[i], False)], axis=1))
                   for i in n_it]
            t = [t[i] + res[i][:, 0:LANES] for i in n_it]
            m = [res[i][:, LANES:2 * LANES] for i in n_it]
        else:
            t = [t[i] + _mm(m[i].astype(BF16), bd(t[i], False)) for i in n_it]
    x = [_mm(t[i].astype(BF16), bd(x[i], True)) for i in n_it]
    u_loc = [x[i][:, 0:LANES] for i in n_it]
    z = [x[i][:, LANES:2 * LANES] for i in n_it]
    yg = [_mm(jnp.concatenate([a_rk[i], -a_rb[i]], axis=1).astype(BF16),
              jnp.concatenate([jnp.concatenate([v_bd[i], zeros_sq], axis=1), bd(x[i], True)], axis=0))
          for i in n_it]
    y_loc = [yg[i][:, 0:LANES] for i in n_it]
    g = [(r_t[i] + yg[i][:, LANES:2 * LANES]).astype(BF16) for i in n_it]
    q = [_tn(jnp.concatenate([v2[i], u_loc[i]], axis=0).astype(BF16),
             jnp.concatenate([blk(s_kw, items[i]), -blk(s_bw, items[i])], axis=0).astype(BF16)) * blockdiag
         for i in n_it]
    pm = [(_tn(z[i].astype(BF16), blk(s_bw, items[i]).astype(BF16)) * blockdiag).astype(BF16) for i in n_it]

    for p in range(RWKV_PAIRS):
        st = state[p]
        for c in range(n_ch):
            i = c * RWKV_PAIRS + p
            sb = st.astype(BF16)
            s_y[rows(c), cols(p)] = y_loc[i] + _nt(g[i], sb)
            st = st * jnp.exp(ends[c][:, cols(p)]) - _mm(sb, pm[i]) + q[i]
        state[p] = st

    y = s_y[...]
    inv_n = 1.0 / hd
    mu = _mm_split(y, ones_ref[...]) * inv_n
    dev = y - mu
    var = _mm((dev * dev).astype(BF16), ones_ref[...]) * inv_n
    yn = dev * lax.rsqrt(var + RWKV_NORM_EPS) * gn_ref[...]
    o_ref[...] = ((yn + s_bonus[...]) * s_g[...]).astype(o_ref.dtype)


def _rwkv(pg, mu, w0, w_lora, a0, a_lora, g_lora, k_k, k_a, r_k, gn_g, bsz, seq):
    tb = min(RWKV_TB, seq)
    nt = seq // tb
    mw = MIX_WIDTH
    col = lambda cb: pl.BlockSpec((tb, mw), lambda b, i, cb=cb: (b * nt + i, cb))
    c0 = COL_RWKV // mw
    const = lambda shape: pl.BlockSpec(shape, lambda b, i: (0,) * len(shape))
    row = lambda t: t.reshape(1, -1)
    idx = jnp.arange(tb)
    same_chunk = (idx[:, None] // RWKV_CHUNK) == (idx[None, :] // RWKV_CHUNK)
    tri = ((idx[:, None] >= idx[None, :]) & same_chunk).astype(BF16)
    hid = jnp.arange(mw) // RWKV_HEAD_DIM
    ones = (hid[:, None] == hid[None, :]).astype(BF16)
    vec = const((1, mw))
    buf = pltpu.VMEM((tb, mw), F32)
    return pl.pallas_call(
        functools.partial(_rwkv_kernel, tb=tb),
        grid=(bsz, nt),
        in_specs=[col(c0), col(c0 + 1), col(c0 + 2),
                  pl.BlockSpec((tb, LORA_COLS), lambda b, i: (b * nt + i, COL_LORA // LORA_COLS)),
                  vec, vec, vec, const((1, LORA_COLS)), vec, vec,
                  const((DECAY_LORA, mw)), const((AAA_LORA, mw)), const((GATE_LORA, mw)),
                  vec, vec, vec, vec, const((tb, tb)), const((mw, mw))],
        out_specs=pl.BlockSpec((tb, mw), lambda b, i: (b * nt + i, 0)),
        out_shape=jax.ShapeDtypeStruct((bsz * seq, mw), BF16),
        scratch_shapes=[pltpu.VMEM((SUBLANES, mw), F32)] * 3 + [pltpu.VMEM((SUBLANES, LORA_COLS), F32),
                        pltpu.VMEM((RWKV_PAIRS, LANES, LANES), F32)] + [buf] * 10,
        compiler_params=_params(("arbitrary", "arbitrary")),
        name="rwkv",
    )(pg, pg, pg, pg, row(mu[0:mw]), row(mu[mw:2 * mw]), row(mu[2 * mw:3 * mw]), row(mu[3 * mw:]),
      row(w0), row(a0), w_lora.astype(BF16), a_lora.astype(BF16), g_lora.astype(BF16),
      row(k_k), row(k_a), row(r_k), row(gn_g), tri, ones)


def _merge_kernel(ch_ref, cb_ref, cc_ref, g0_ref, g1_ref, g2_ref, yret_ref, yrwkv_ref, x_ref, convw_ref,
                  wb_ref, wo_ref, gate1_ref, n2g_ref, shift2_ref, scale2_ref, wr_ref, br_ref, tri_ref,
                  x1_ref, h2_ref, route_ref, cnt_ref, ucarry, cnt_scr, *, tm):
    first_step = (pl.program_id(0) == 0) & (pl.program_id(1) == 0)

    @pl.when(first_step)
    def _():
        cnt_scr[...] = jnp.zeros_like(cnt_scr)

    @pl.when(pl.program_id(1) == 0)
    def _():
        ucarry[...] = jnp.zeros_like(ucarry)

    u = cc_ref[...].astype(F32) * ch_ref[...].astype(F32)
    row = lax.broadcasted_iota(jnp.int32, u.shape, 0)
    last = ucarry[SUBLANES - 1:SUBLANES, :]
    u1 = jnp.where(row == 0, last, pltpu.roll(u, 1, 0))
    u2 = jnp.where(row == 0, ucarry[SUBLANES - 2:SUBLANES - 1, :], jnp.where(row == 1, last, pltpu.roll(u, 2, 0)))
    ucarry[...] = u[tm - SUBLANES:tm, :]
    y_conv = cb_ref[...].astype(F32) * (convw_ref[0:1, :] * u2 + convw_ref[1:2, :] * u1 + convw_ref[2:3, :] * u)

    merged = g0_ref[...].astype(F32) * _mm(yret_ref[...], wb_ref[0])
    merged = merged + g1_ref[...].astype(F32) * _mm(yrwkv_ref[...], wb_ref[1])
    merged = merged + g2_ref[...].astype(F32) * _mm(y_conv.astype(BF16), wb_ref[2])
    x1 = x_ref[...] + gate1_ref[0] * _mm(merged.astype(BF16), wo_ref[...])
    x1_ref[...] = x1
    h2 = _norm_mod(x1, n2g_ref[...], shift2_ref[0], scale2_ref[0])
    bits = lax.bitcast_convert_type(h2.astype(BF16).astype(F32), jnp.uint32)
    half = bits.shape[1] // 2
    h2_ref[...] = (bits[:, :half] >> 16) | (bits[:, half:] & jnp.uint32(0xFFFF0000))

    h_hi = h2.astype(BF16)
    h_lo = (h2 - h_hi.astype(F32)).astype(BF16)
    logits = _mm(h_hi, wr_ref[0]) + _mm(h_lo, wr_ref[0]) + _mm(h_hi, wr_ref[1]) + br_ref[...]
    lane = lax.broadcasted_iota(jnp.int32, logits.shape, 1)
    lanef = lane.astype(F32)
    neg = -jnp.inf
    big = 1e9
    is_group = lane < N_GROUPS
    gl = jnp.where(is_group, logits, neg)
    gmax = jnp.max(gl, axis=-1, keepdims=True)
    gsel = jnp.min(jnp.where(gl == gmax, lanef, big), axis=-1, keepdims=True)
    p_group = 1.0 / jnp.sum(jnp.where(is_group, jnp.exp(gl - gmax), 0.0), axis=-1, keepdims=True)
    lo = ROUTE_LANE0 + EXPERTS_PER_GROUP * gsel
    el = jnp.where((lanef >= lo) & (lanef < lo + EXPERTS_PER_GROUP), logits, neg)
    m1 = jnp.max(el, axis=-1, keepdims=True)
    i1 = jnp.min(jnp.where(el == m1, lanef, big), axis=-1, keepdims=True)
    el2 = jnp.where(lanef == i1, neg, el)
    m2 = jnp.max(el2, axis=-1, keepdims=True)
    i2 = jnp.min(jnp.where(el2 == m2, lanef, big), axis=-1, keepdims=True)
    t = jnp.exp(m2 - m1)
    w1 = p_group / (1.0 + t)
    w2 = p_group * t / (1.0 + t)

    hit1 = lanef == i1
    hit2 = lanef == i2
    onehot = jnp.where(hit1 | hit2, 1.0, 0.0)
    before = _mm(tri_ref[...], onehot.astype(BF16)) + cnt_scr[0:1, :]
    rank1 = jnp.sum(jnp.where(hit1, before, 0.0), axis=-1, keepdims=True)
    rank2 = jnp.sum(jnp.where(hit2, before, 0.0), axis=-1, keepdims=True)
    cnt_scr[...] = cnt_scr[...] + jnp.sum(onehot, axis=0, keepdims=True)
    cnt_ref[...] = cnt_scr[...]

    out = jnp.where(lane == 0, i1 - ROUTE_LANE0, 0.0)
    out = jnp.where(lane == 1, i2 - ROUTE_LANE0, out)
    out = jnp.where(lane == 2, w1, out)
    out = jnp.where(lane == 3, w2, out)
    out = jnp.where(lane == 4, rank1, out)
    out = jnp.where(lane == 5, rank2, out)
    route_ref[...] = out


def _merge(pg, y_ret, y_rwkv, x, conv_w, w_branch, w_o, gate1, norm2_g, shift2, scale2, w_route, b_route,
           bsz, seq):
    tm = min(MERGE_TM, seq)
    nt = seq // tm
    d = D_MODEL
    mw = MIX_WIDTH
    n_tok = bsz * seq
    tok = lambda width, cb=0: pl.BlockSpec((tm, width), lambda b, i, cb=cb: (b * nt + i, cb))
    const = lambda shape: pl.BlockSpec(shape, lambda b, i: (0,) * len(shape))
    per_b = pl.BlockSpec((1, 1, d), lambda b, i: (b, 0, 0))
    cc0 = COL_CONV // mw
    gc0 = COL_GATE // d
    idx = jnp.arange(tm)
    tri = (idx[:, None] > idx[None, :]).astype(BF16)
    convw = jnp.zeros((SUBLANES, mw), F32).at[0:conv_w.shape[0]].set(conv_w)
    return pl.pallas_call(
        functools.partial(_merge_kernel, tm=tm),
        grid=(bsz, nt),
        in_specs=[tok(mw, cc0), tok(mw, cc0 + 1), tok(mw, cc0 + 2),
                  tok(d, gc0), tok(d, gc0 + 1), tok(d, gc0 + 2),
                  tok(mw), tok(mw), tok(d), const((SUBLANES, mw)),
                  const((N_BRANCHES, mw, d)), const((d, d)), per_b, const((1, d)), per_b, per_b,
                  const((2, d, LANES)), const((1, LANES)), const((tm, tm))],
        out_specs=[tok(d), tok(d // 2), tok(LANES), const((SUBLANES, LANES))],
        out_shape=[jax.ShapeDtypeStruct((n_tok, d), F32), jax.ShapeDtypeStruct((n_tok, d // 2), jnp.uint32),
                   jax.ShapeDtypeStruct((n_tok, LANES), F32), jax.ShapeDtypeStruct((SUBLANES, LANES), F32)],
        scratch_shapes=[pltpu.VMEM((SUBLANES, mw), F32), pltpu.VMEM((SUBLANES, LANES), F32)],
        compiler_params=_params(("arbitrary", "arbitrary")),
        name="merge",
    )(pg, pg, pg, pg, pg, pg, y_ret, y_rwkv, x, convw, w_branch, w_o, gate1, norm2_g, shift2, scale2,
      jnp.stack([w_route.astype(BF16), (w_route - w_route.astype(BF16).astype(F32)).astype(BF16)]), b_route, tri)


def _row_copy(src, src_row, dst, dst_row, sem):
    return pltpu.make_async_copy(src.at[pl.ds(src_row, 1), :], dst.at[pl.ds(dst_row, 1), :], sem)


def _dispatch_kernel(dest_ref, h2_ref, rows_in_ref, rows_ref, sem, *, tile):
    del rows_in_ref

    def wait(r, carry):
        for k in range(2):
            _row_copy(h2_ref, 0, rows_ref, 0, sem).wait()
        return carry

    for r in range(tile):
        for k in range(2):
            _row_copy(h2_ref, r, rows_ref, dest_ref[0, 0, 2 * r + k], sem).start(priority=k)
    lax.fori_loop(0, tile, wait, 0, unroll=8)


def _dispatch(dest, h2, n_rows):
    n_tok, d = h2.shape
    tile = min(ROW_TILE, n_tok)
    nt = n_tok // tile
    zeros = jnp.zeros((n_rows, d), h2.dtype)
    return pl.pallas_call(
        functools.partial(_dispatch_kernel, tile=tile),
        grid=(nt,),
        in_specs=[pl.BlockSpec((1, 1, 2 * tile), lambda i: (i, 0, 0), memory_space=pltpu.SMEM),
                  pl.BlockSpec((tile, d), lambda i: (i, 0)),
                  pl.BlockSpec(memory_space=pl.ANY)],
        out_specs=pl.BlockSpec(memory_space=pl.ANY),
        out_shape=jax.ShapeDtypeStruct((n_rows, d), h2.dtype),
        scratch_shapes=[pltpu.SemaphoreType.DMA(())],
        input_output_aliases={2: 0},
        compiler_params=_params(("arbitrary",)),
        name="dispatch",
    )(dest.reshape(nt, 1, 2 * tile), h2, zeros)


def _expert_kernel(be_ref, nused_ref, x_ref, wg_ref, wu_ref, wd_ref, y_ref, wg_s, wu_s, wd_s):
    b = pl.program_id(0)
    used = b < nused_ref[0]

    @pl.when((b == 0) | (be_ref[b] != be_ref[jnp.maximum(b - 1, 0)]))
    def _():
        wg_s[...] = wg_ref[0].astype(BF16)
        wu_s[...] = wu_ref[0].astype(BF16)
        wd_s[...] = wd_ref[0].astype(BF16)

    @pl.when(used)
    def _():
        w = x_ref[...]
        xb = jnp.concatenate([lax.bitcast_convert_type(w << 16, F32),
                              lax.bitcast_convert_type(w & jnp.uint32(0xFFFF0000), F32)], axis=1).astype(BF16)
        g = _mm(xb, wg_s[...])
        u = _mm(xb, wu_s[...])
        y_ref[...] = _mm(((g * jax.nn.sigmoid(g)) * u).astype(BF16), wd_s[...])

    @pl.when(jnp.logical_not(used))
    def _():
        y_ref[...] = jnp.zeros_like(y_ref)


def _experts(block_e, n_used, x_rows, w_gate, w_up, w_down):
    n_rows = x_rows.shape[0]
    d, ff = w_gate.shape[-2:]
    grid_spec = pltpu.PrefetchScalarGridSpec(
        num_scalar_prefetch=2,
        grid=(n_rows // EXPERT_ROWS,),
        in_specs=[pl.BlockSpec((EXPERT_ROWS, d // 2), lambda b, be, nu: (b, 0)),
                  pl.BlockSpec((1, d, ff), lambda b, be, nu: (be[b], 0, 0)),
                  pl.BlockSpec((1, d, ff), lambda b, be, nu: (be[b], 0, 0)),
                  pl.BlockSpec((1, ff, d), lambda b, be, nu: (be[b], 0, 0))],
        out_specs=pl.BlockSpec((EXPERT_ROWS, d), lambda b, be, nu: (b, 0)),
        scratch_shapes=[pltpu.VMEM((d, ff), BF16), pltpu.VMEM((d, ff), BF16), pltpu.VMEM((ff, d), BF16)],
    )
    return pl.pallas_call(
        _expert_kernel,
        grid_spec=grid_spec,
        out_shape=jax.ShapeDtypeStruct((n_rows, d), F32),
        compiler_params=_params(("arbitrary",)),
        name="expert",
    )(block_e, n_used, x_rows, w_gate, w_up, w_down)


def _combine_kernel(dest_ref, dest_next_ref, x1_ref, route_ref, gate2_ref, fg_ref, rows_ref, o_ref, ybuf, sem,
                    *, tile, final, n_tiles):
    i = pl.program_id(0)
    slot = i % 2

    def issue(dref, s):
        for r in range(tile):
            for k in range(2):
                _row_copy(rows_ref, dref[0, 0, 2 * r + k], ybuf.at[s, k], r, sem.at[s]).start(priority=k)

    @pl.when(i == 0)
    def _():
        issue(dest_ref, 0)

    for s in range(2):
        @pl.when((i + 1 < n_tiles) & (slot == 1 - s))
        def _(s=s):
            issue(dest_next_ref, s)

    def wait(r, carry):
        for k in range(2):
            _row_copy(rows_ref, 0, ybuf.at[slot, k], 0, sem.at[slot]).wait()
        return carry

    lax.fori_loop(0, tile, wait, 0, unroll=8)
    route = route_ref[...]
    y = route[:, 2:3] * ybuf[slot, 0] + route[:, 3:4] * ybuf[slot, 1]
    x2 = x1_ref[...] + gate2_ref[0] * y
    if final:
        ms = jnp.mean(x2 * x2, axis=-1, keepdims=True)
        x2 = x2 * lax.rsqrt(ms + NORM_EPS) * fg_ref[...]
    o_ref[...] = x2


def _combine(dest, x1, route, gate2, final_g, y_rows, seq, final):
    n_tok, d = x1.shape
    tile = min(ROW_TILE, seq)
    nt = n_tok // tile
    per_seq = seq // tile
    return pl.pallas_call(
        functools.partial(_combine_kernel, tile=tile, final=final, n_tiles=nt),
        grid=(nt,),
        in_specs=[pl.BlockSpec((1, 1, 2 * tile), lambda i: (i, 0, 0), memory_space=pltpu.SMEM),
                  pl.BlockSpec((1, 1, 2 * tile), lambda i: (jnp.minimum(i + 1, nt - 1), 0, 0),
                               memory_space=pltpu.SMEM),
                  pl.BlockSpec((tile, d), lambda i: (i, 0)),
                  pl.BlockSpec((tile, LANES), lambda i: (i, 0)),
                  pl.BlockSpec((1, 1, d), lambda i: (i // per_seq, 0, 0)),
                  pl.BlockSpec((1, d), lambda i: (0, 0)),
                  pl.BlockSpec(memory_space=pl.ANY)],
        out_specs=pl.BlockSpec((tile, d), lambda i: (i, 0)),
        out_shape=jax.ShapeDtypeStruct((n_tok, d), F32),
        scratch_shapes=[pltpu.VMEM((2, 2, tile, d), F32), pltpu.SemaphoreType.DMA((2,))],
        compiler_params=_params(("arbitrary",)),
        name="combine",
    )(dest.reshape(nt, 1, 2 * tile), dest.reshape(nt, 1, 2 * tile), x1, route, gate2, final_g, y_rows)


def _moe(x1, h2, route, counts, gate2, final_g, w_gate, w_up, w_down, seq, final, layer):
    n_tok = x1.shape[0]
    n_rows = 2 * n_tok + N_EXPERTS * EXPERT_ROWS
    n_blocks = n_rows // EXPERT_ROWS
    cnt = counts[0, ROUTE_LANE0:ROUTE_LANE0 + N_EXPERTS].astype(jnp.int32)
    padded = (cnt + EXPERT_ROWS - 1) // EXPERT_ROWS * EXPERT_ROWS
    pends = jnp.cumsum(padded)
    pstarts = pends - padded
    eid = route[:, 0:2].astype(jnp.int32)
    dest = (pstarts[eid] + route[:, 4:6].astype(jnp.int32)).reshape(-1)
    n_used = (pends[-1] // EXPERT_ROWS).astype(jnp.int32)
    blk = jnp.minimum(jnp.arange(n_blocks, dtype=jnp.int32), n_used - 1)
    block_e = jnp.sum((pends[None, :] <= (blk * EXPERT_ROWS)[:, None]).astype(jnp.int32), axis=1)
    block_e = jnp.minimum(block_e, N_EXPERTS - 1)
    x_rows = _dispatch(dest, h2, n_rows)
    stacked = lambda w: w.reshape((-1,) + w.shape[2:])
    y_rows = _experts(block_e + layer * N_EXPERTS, n_used.reshape(1), x_rows,
                      stacked(w_gate), stacked(w_up), stacked(w_down))
    return _combine(dest, x1, route, gate2, final_g, y_rows, seq, final)


def kernel(x, c, positions, norm1_g, norm2_g, final_norm_g, w_ada, b_ada, w_in, w_gate, b_gate, ret_gn_g, rwkv_mu, rwkv_w0, rwkv_w_lora, rwkv_a0, rwkv_a_lora, rwkv_g_lora, rwkv_k_k, rwkv_k_a, rwkv_r_k, rwkv_gn_g, conv_w, w_branch, w_o, w_router_group, b_router_group, w_router_expert, b_router_expert, w_exp_gate, w_exp_up, w_exp_down):
    bsz, seq, d = x.shape
    assert d == D_MODEL and seq % RET_TB == 0
    n_layers = w_in.shape[0]
    n_tok = bsz * seq
    xt = x.reshape(n_tok, d)
    mod = _ada(c, w_ada, b_ada)
    cosf, sinf = _rope_tables(positions)
    ret_tables = _ret_tables()
    for l in range(n_layers):
        shift1, scale1, gate1, shift2, scale2, gate2 = [
            mod[l, :, i * d:(i + 1) * d].reshape(bsz, 1, d) for i in range(6)]
        o_rwkv = RET_COLS
        o_conv = RET_COLS + RWKV_COLS
        w_cat = jnp.concatenate([w_gate[l], w_in[l][:, :RET_COLS], w_in[l][:, o_conv:], w_in[l][:, o_rwkv:o_conv]],
                                axis=1).astype(BF16)
        b_cat = jnp.concatenate([b_gate[l], jnp.zeros((IN_COLS,), F32)])[None, :]
        pg = _proj(xt, norm1_g[l][None, :], shift1, scale1, w_cat, b_cat, bsz, seq)
        y_ret = _retention(pg, cosf, sinf, ret_tables, ret_gn_g[l][None, :], bsz, seq)
        y_rwkv = _rwkv(pg, rwkv_mu[l], rwkv_w0[l], rwkv_w_lora[l], rwkv_a0[l], rwkv_a_lora[l], rwkv_g_lora[l],
                       rwkv_k_k[l], rwkv_k_a[l], rwkv_r_k[l].reshape(-1), rwkv_gn_g[l], bsz, seq)
        w_route = jnp.zeros((d, LANES), F32)
        w_route = w_route.at[:, 0:N_GROUPS].set(w_router_group[l])
        w_route = w_route.at[:, ROUTE_LANE0:ROUTE_LANE0 + N_EXPERTS].set(w_router_expert[l])
        b_route = jnp.zeros((1, LANES), F32)
        b_route = b_route.at[0, 0:N_GROUPS].set(b_router_group[l])
        b_route = b_route.at[0, ROUTE_LANE0:ROUTE_LANE0 + N_EXPERTS].set(b_router_expert[l])
        x1, h2, route, counts = _merge(pg, y_ret, y_rwkv, xt, conv_w[l], w_branch[l].astype(BF16),
                                       w_o[l].astype(BF16), gate1, norm2_g[l][None, :], shift2, scale2,
                                       w_route, b_route, bsz, seq)
        xt = _moe(x1, h2, route, counts, gate2, final_norm_g[None, :], w_exp_gate, w_exp_up, w_exp_down,
                  seq, final=(l == n_layers - 1), layer=l)
    return xt.reshape(bsz, seq, d)
```
